```python
import math
import jax, jax.numpy as jnp
from jax import lax
import numpy as np

D_MODEL = 1024
BATCH = 4
SEQ = 4096
DEPTH = 1

DA_HEADS = 8
DA_HEAD_DIM = 64
DA_V_DIM = 2 * DA_HEAD_DIM
DA_QK_WIDTH = DA_HEADS * 2 * DA_HEAD_DIM
DA_WIDTH = DA_HEADS * DA_V_DIM
ROPE_DIM = DA_HEAD_DIM // 4
ROPE_THETA = 500000.0
Q_BLOCK = 128
ML_HEADS = 8
ML_QK_DIM = 64
ML_V_DIM = 128
ML_QK_WIDTH = ML_HEADS * ML_QK_DIM
ML_V_WIDTH = ML_HEADS * ML_V_DIM
CONV_WIDTH = 4
CHUNK = 64
N_BRANCH = 2
D_FF = 4 * D_MODEL
EPS = 1e-6
SPLIT_SIZES = (DA_QK_WIDTH, DA_QK_WIDTH, DA_WIDTH,
               ML_QK_WIDTH, ML_QK_WIDTH, ML_V_WIDTH, ML_HEADS, ML_HEADS, ML_V_WIDTH,
               N_BRANCH * D_MODEL)
D_IN = sum(SPLIT_SIZES)

kernel_name = "hybrid_diffattn_mlstm_gated_block"


def rms_norm(x, w):
    x32 = x.astype(jnp.float32)
    y = x32 * lax.rsqrt(jnp.mean(x32 * x32, axis=-1, keepdims=True) + EPS)
    return (y * w.astype(jnp.float32)).astype(x.dtype)


def rope_tables(positions):
    inv = ROPE_THETA ** (-jnp.arange(0, ROPE_DIM, 2, dtype=jnp.float32) / ROPE_DIM)
    ang = positions.astype(jnp.float32)[..., None] * inv
    return jnp.cos(ang), jnp.sin(ang)


def apply_partial_rope(x, cos, sin):
    half = ROPE_DIM // 2
    xr = x[..., :ROPE_DIM].astype(jnp.float32)
    x1, x2 = xr[..., :half], xr[..., half:]
    c = cos[:, :, None, None, :]
    s = sin[:, :, None, None, :]
    rot = jnp.concatenate([x1 * c - x2 * s, x2 * c + x1 * s], axis=-1)
    return jnp.concatenate([rot.astype(x.dtype), x[..., ROPE_DIM:]], axis=-1)


def diff_attention(q, k, v, lam, cos, sin):
    B, S = q.shape[0], q.shape[1]
    H, d, dv = DA_HEADS, DA_HEAD_DIM, DA_V_DIM
    q = apply_partial_rope(q, cos, sin).astype(jnp.float32) * (d ** -0.5)
    k = apply_partial_rope(k, cos, sin).astype(jnp.float32)
    q = q.transpose(0, 2, 3, 1, 4)
    k = k.transpose(0, 2, 3, 1, 4)
    v32 = v.astype(jnp.float32).transpose(0, 2, 1, 3)
    nblk = S // Q_BLOCK
    qb = jnp.moveaxis(q.reshape(B, H, 2, nblk, Q_BLOCK, d), 3, 0)
    kpos = jnp.arange(S)

    def block(args):
        q_blk, j = args
        s = jnp.einsum('bhcqd,bhckd->bhcqk', q_blk, k)
        qpos = j * Q_BLOCK + jnp.arange(Q_BLOCK)
        mask = kpos[None, :] <= qpos[:, None]
        p = jax.nn.softmax(jnp.where(mask, s, -jnp.inf), axis=-1)
        a = p[:, :, 0] - lam * p[:, :, 1]
        return jnp.einsum('bhqk,bhkv->bhqv', a, v32)

    o = lax.map(block, (qb, jnp.arange(nblk)))
    o = jnp.moveaxis(o, 0, 2).reshape(B, H, S, dv)
    return o.transpose(0, 2, 1, 3)


def causal_dwconv(u, w, b):
    S = u.shape[1]
    up = jnp.pad(u, ((0, 0), (CONV_WIDTH - 1, 0), (0, 0)))
    out = b
    for j in range(CONV_WIDTH):
        out = out + up[:, j:j + S] * w[j]
    return out


def mlstm_chunkwise(q, k, v, i_pre, f_pre):
    B, H, S, dk = q.shape
    dv = v.shape[-1]
    L = CHUNK
    NC = S // L
    q = q * (dk ** -0.5)
    logf = jax.nn.log_sigmoid(f_pre)

    def chunks(a):
        return jnp.moveaxis(a.reshape((B, H, NC, L) + a.shape[3:]), 2, 0)

    qc, kc, vc, ic = chunks(q), chunks(k), chunks(v), chunks(i_pre)
    bc = jnp.cumsum(chunks(logf), axis=-1)
    tri = jnp.tril(jnp.ones((L, L), dtype=bool))

    def step(carry, inp):
        C, n, m = carry
        qb, kb, vb, ib, bb = inp
        D = jnp.where(tri, bb[..., :, None] - bb[..., None, :] + ib[..., None, :], -jnp.inf)
        m_inter = bb + m[..., None]
        m_t = jnp.maximum(jnp.max(D, axis=-1), m_inter)
        dexp = jnp.exp(D - m_t[..., None])
        inter = jnp.exp(m_inter - m_t)
        s = jnp.einsum('bhtd,bhsd->bhts', qb, kb) * dexp
        num = jnp.einsum('bhts,bhsv->bhtv', s, vb) + inter[..., None] * jnp.einsum('bhtd,bhdv->bhtv', qb, C)
        den = jnp.sum(s, axis=-1) + inter * jnp.einsum('bhtd,bhd->bht', qb, n)
        h = num / jnp.maximum(jnp.abs(den), jnp.exp(-m_t))[..., None]
        b_last = bb[..., -1]
        w_log = b_last[..., None] - bb + ib
        m_new = jnp.maximum(b_last + m, jnp.max(w_log, axis=-1))
        decay = jnp.exp(b_last + m - m_new)
        ws = jnp.exp(w_log - m_new[..., None])
        C_new = decay[..., None, None] * C + jnp.einsum('bhs,bhsd,bhsv->bhdv', ws, kb, vb)
        n_new = decay[..., None] * n + jnp.einsum('bhs,bhsd->bhd', ws, kb)
        return (C_new, n_new, m_new), h

    init = (jnp.zeros((B, H, dk, dv), jnp.float32), jnp.zeros((B, H, dk), jnp.float32),
            jnp.zeros((B, H), jnp.float32))
    _, hs = lax.scan(step, init, (qc, kc, vc, ic, bc))
    return jnp.moveaxis(hs, 0, 2).reshape(B, H, S, dv)


def setup_inputs(seed: int = 0) -> dict:
    key = jax.random.key(seed)
    ks = jax.random.split(key, 20)
    f32 = jnp.float32
    nrm = lambda k, shape, scale: jax.random.normal(k, shape, f32) * scale
    x = jax.random.normal(ks[0], (BATCH, SEQ, D_MODEL), f32)
    positions = jnp.broadcast_to(jnp.arange(SEQ, dtype=jnp.int32), (BATCH, SEQ))
    gate_b_i = nrm(ks[1], (DEPTH, ML_HEADS), 0.1)
    gate_b_f = jnp.linspace(3.0, 6.0, ML_HEADS, dtype=f32)[None, :] + nrm(ks[2], (DEPTH, ML_HEADS), 0.1)
    return {
        "x": x,
        "positions": positions,
        "norm_mix_w": 1.0 + nrm(ks[3], (DEPTH, D_MODEL), 0.02),
        "w_in": nrm(ks[4], (DEPTH, D_MODEL, D_IN), D_MODEL ** -0.5),
        "ml_gate_b": jnp.concatenate([gate_b_i, gate_b_f], axis=-1),
        "conv_w": nrm(ks[5], (DEPTH, CONV_WIDTH, 2 * ML_QK_WIDTH), CONV_WIDTH ** -0.5),
        "conv_b": nrm(ks[6], (DEPTH, 2 * ML_QK_WIDTH), 0.01),
        "da_lambda": nrm(ks[7], (DEPTH, 4, DA_HEAD_DIM), 0.1),
        "da_subln_w": 1.0 + nrm(ks[8], (DEPTH, DA_V_DIM), 0.02),
        "ml_norm_w": 1.0 + nrm(ks[9], (DEPTH, ML_V_WIDTH), 0.02),
        "w_proj_a": nrm(ks[10], (DEPTH, DA_WIDTH, D_MODEL), DA_WIDTH ** -0.5),
        "w_proj_b": nrm(ks[11], (DEPTH, ML_V_WIDTH, D_MODEL), ML_V_WIDTH ** -0.5),
        "w_out": nrm(ks[12], (DEPTH, D_MODEL, D_MODEL), D_MODEL ** -0.5),
        "norm_ffn_w": 1.0 + nrm(ks[13], (DEPTH, D_MODEL), 0.02),
        "w_ff1": nrm(ks[14], (DEPTH, D_MODEL, D_FF), D_MODEL ** -0.5),
        "w_ff2": nrm(ks[15], (DEPTH, D_FF, D_MODEL), D_FF ** -0.5),
        "final_norm_w": 1.0 + nrm(ks[16], (D_MODEL,), 0.02),
    }


def reference(x, positions, norm_mix_w, w_in, ml_gate_b, conv_w, conv_b, da_lambda,
              da_subln_w, ml_norm_w, w_proj_a, w_proj_b, w_out, norm_ffn_w, w_ff1,
              w_ff2, final_norm_w):
    B, S, _ = x.shape
    cos, sin = rope_tables(positions)
    split_idx = [int(c) for c in np.cumsum(SPLIT_SIZES)[:-1]]
    h = x
    for l in range(DEPTH):
        lambda_init = 0.8 - 0.6 * math.exp(-0.3 * l)
        xn = rms_norm(h, norm_mix_w[l])
        proj = jnp.einsum('bsd,de->bse', xn, w_in[l])
        (da_q, da_k, da_v, ml_q, ml_k, ml_v, ml_i, ml_f, ml_o, gate_pre) = jnp.split(proj, split_idx, axis=-1)

        lam_p = da_lambda[l].astype(jnp.float32)
        lam = (jnp.exp(jnp.sum(lam_p[0] * lam_p[1])) - jnp.exp(jnp.sum(lam_p[2] * lam_p[3]))
               + lambda_init)
        qa = da_q.reshape(B, S, DA_HEADS, 2, DA_HEAD_DIM)
        ka = da_k.reshape(B, S, DA_HEADS, 2, DA_HEAD_DIM)
        va = da_v.reshape(B, S, DA_HEADS, DA_V_DIM)
        oa = diff_attention(qa, ka, va, lam, cos, sin)
        oa = rms_norm(oa, da_subln_w[l]) * (1.0 - lambda_init)
        ya = jnp.einsum('bse,ed->bsd', oa.reshape(B, S, DA_WIDTH).astype(x.dtype), w_proj_a[l])

        qk = jax.nn.silu(causal_dwconv(jnp.concatenate([ml_q, ml_k], axis=-1), conv_w[l], conv_b[l]))
        qm = qk[..., :ML_QK_WIDTH].reshape(B, S, ML_HEADS, ML_QK_DIM).transpose(0, 2, 1, 3)
        km = qk[..., ML_QK_WIDTH:].reshape(B, S, ML_HEADS, ML_QK_DIM).transpose(0, 2, 1, 3)
        vm = ml_v.reshape(B, S, ML_HEADS, ML_V_DIM).transpose(0, 2, 1, 3)
        i_pre = (ml_i + ml_gate_b[l, :ML_HEADS]).transpose(0, 2, 1)
        f_pre = (ml_f + ml_gate_b[l, ML_HEADS:]).transpose(0, 2, 1)
        hm = mlstm_chunkwise(qm.astype(jnp.float32), km.astype(jnp.float32), vm.astype(jnp.float32),
                             i_pre.astype(jnp.float32), f_pre.astype(jnp.float32))
        hm = hm.transpose(0, 2, 1, 3)
        hm = rms_norm(hm, ml_norm_w[l].reshape(ML_HEADS, ML_V_DIM)).reshape(B, S, ML_V_WIDTH)
        ob = (jax.nn.sigmoid(ml_o.astype(jnp.float32)) * hm).astype(x.dtype)
        yb = jnp.einsum('bse,ed->bsd', ob, w_proj_b[l])

        g = jax.nn.sigmoid(gate_pre.astype(jnp.float32)).reshape(B, S, N_BRANCH, D_MODEL).astype(x.dtype)
        merged = g[:, :, 0] * ya + g[:, :, 1] * yb
        h = h + jnp.einsum('bsd,de->bse', merged, w_out[l])

        hn = rms_norm(h, norm_ffn_w[l])
        u = jax.nn.relu(jnp.einsum('bsd,df->bsf', hn, w_ff1[l]))
        h = h + jnp.einsum('bsf,fd->bsd', u * u, w_ff2[l])
    return rms_norm(h, final_norm_w)
```

```python
import functools
import math

import jax
import jax.numpy as jnp
from jax import lax
from jax.experimental import pallas as pl
from jax.experimental.pallas import tpu as pltpu

F32 = jnp.float32
BF16 = jnp.bfloat16

D_MODEL = 1024
DA_HEADS = 8
DA_HEAD_DIM = 64
DA_V_DIM = 128
ROPE_DIM = 16
ROPE_THETA = 500000.0
ML_HEADS = 8
ML_QK_DIM = 64
ML_V_DIM = 128
CONV_WIDTH = 4
D_FF = 4 * D_MODEL
EPS = 1e-6

LANES = 128
SUBLANES = 8
VMEM_LIMIT_BYTES = 56 * 1024 * 1024

PROJ_TM = 1024
PROJ_TN = 1024
ATT_TQ = 256
ATT_TK = 256
ML_CHUNK = 128
POST_TM = 512
FF_CHUNK = 1024

SEG_Q, SEG_K, SEG_V, SEG_MQK, SEG_MV, SEG_MO, SEG_G0, SEG_G1 = range(8)
N_SEG = 8

NEG_BIG = -1e30
LOG2E = 1.4426950408889634


def _dot(a, b):
    return jnp.dot(a, b, preferred_element_type=F32)


def _dot_nt(a, b):
    return lax.dot_general(a, b, (((1,), (1,)), ((), ())), preferred_element_type=F32)


def _dot_tn(a, b):
    return lax.dot_general(a, b, (((0,), (0,)), ((), ())), preferred_element_type=F32)


def _sigmoid(x):
    return 1.0 / (1.0 + jnp.exp(-x))


def _rms(x, w):
    return x * lax.rsqrt(jnp.mean(x * x, axis=-1, keepdims=True) + EPS) * w


def _inproj_kernel(x_ref, pos_ref, nw_ref, invf_ref, w_ref, wg_ref, out_ref, gates_ref,
                   xn_sc, cos_sc, sina_sc, sinb_sc):
    j = pl.program_id(1)

    @pl.when(j == 0)
    def _():
        xn = _rms(x_ref[...], nw_ref[...]).astype(BF16)
        xn_sc[...] = xn
        gates_ref[...] = _dot(xn, wg_ref[...])
        ang = pos_ref[...].astype(F32) * invf_ref[...]
        c = jnp.cos(ang)
        s = jnp.sin(ang)
        lane = lax.broadcasted_iota(jnp.int32, (1, LANES), 1) % DA_HEAD_DIM
        half = ROPE_DIM // 2
        cos_sc[...] = jnp.where(lane < ROPE_DIM, c, 1.0)
        sina_sc[...] = jnp.where(lane < half, -s, 0.0)
        sinb_sc[...] = jnp.where((lane >= half) & (lane < ROPE_DIM), s, 0.0)

    acc = _dot(xn_sc[...], w_ref[...])

    def rope_store(scale):
        cos = cos_sc[...]
        sina = sina_sc[...]
        sinb = sinb_sc[...]
        half = ROPE_DIM // 2
        for c0 in range(0, PROJ_TN, LANES):
            xc = acc[:, c0:c0 + LANES]
            r = (xc * cos + pltpu.roll(xc, LANES - half, 1) * sina
                 + pltpu.roll(xc, half, 1) * sinb)
            out_ref[:, c0:c0 + LANES] = (r * scale).astype(BF16)

    @pl.when(j == SEG_Q)
    def _():
        rope_store(DA_HEAD_DIM ** -0.5 * LOG2E)

    @pl.when(j == SEG_K)
    def _():
        rope_store(1.0)

    @pl.when((j == SEG_V) | (j == SEG_MQK) | (j == SEG_MV))
    def _():
        out_ref[...] = acc.astype(BF16)

    @pl.when(j >= SEG_MO)
    def _():
        out_ref[...] = _sigmoid(acc).astype(BF16)


def _inproj(x2, pos2, nw, invf, wp, wg):
    n = x2.shape[0]
    grid = (n // PROJ_TM, N_SEG)
    return pl.pallas_call(
        _inproj_kernel,
        grid=grid,
        in_specs=[
            pl.BlockSpec((PROJ_TM, D_MODEL), lambda i, j: (i, 0)),
            pl.BlockSpec((PROJ_TM, 1), lambda i, j: (i, 0)),
            pl.BlockSpec((1, D_MODEL), lambda i, j: (0, 0)),
            pl.BlockSpec((1, LANES), lambda i, j: (0, 0)),
            pl.BlockSpec((D_MODEL, PROJ_TN), lambda i, j: (0, j)),
            pl.BlockSpec((D_MODEL, LANES), lambda i, j: (0, 0)),
        ],
        out_specs=[
            pl.BlockSpec((PROJ_TM, PROJ_TN), lambda i, j: (i, j)),
            pl.BlockSpec((PROJ_TM, LANES), lambda i, j: (i, 0)),
        ],
        out_shape=[
            jax.ShapeDtypeStruct((n, N_SEG * PROJ_TN), BF16),
            jax.ShapeDtypeStruct((n, LANES), F32),
        ],
        scratch_shapes=[
            pltpu.VMEM((PROJ_TM, D_MODEL), BF16),
            pltpu.VMEM((PROJ_TM, LANES), F32),
            pltpu.VMEM((PROJ_TM, LANES), F32),
            pltpu.VMEM((PROJ_TM, LANES), F32),
        ],
        compiler_params=pltpu.CompilerParams(
            dimension_semantics=("arbitrary", "arbitrary"),
            vmem_limit_bytes=VMEM_LIMIT_BYTES),
        name="inproj",
    )(x2, pos2, nw, invf, wp, wg)


def _attn_kernel(lam_ref, sw_ref, q_ref, k_ref, v_ref, o_ref, vext_sc, acc_sc, m_sc,
                 *, seq, lambda_init):
    tq, tk = ATT_TQ, ATT_TK
    hd = DA_HEAD_DIM
    nq = seq // tq
    kv_per_q = tq // tk

    vext_sc[:, :DA_V_DIM] = v_ref[...]
    ones_col = (lax.broadcasted_iota(jnp.int32, (seq, LANES), 1) == 0).astype(BF16)
    vext_sc[:, DA_V_DIM:] = ones_col

    lp = lam_ref[...]
    lam = (jnp.exp(jnp.sum(lp[0:1] * lp[1:2], axis=-1, keepdims=True))
           - jnp.exp(jnp.sum(lp[2:3] * lp[3:4], axis=-1, keepdims=True)) + lambda_init)

    def q_block(qi, carry):
        q0row = pl.multiple_of(qi * tq, tq)
        q = q_ref[pl.ds(q0row, tq), :]
        qa = q[:, :hd]
        qb = q[:, hd:]
        m_sc[...] = jnp.full((2 * tq, 1), NEG_BIG, F32)
        acc_sc[...] = jnp.zeros((2 * tq, 2 * LANES), F32)

        def kv_step(j, masked):
            k0row = pl.multiple_of(j * tk, tk)
            kb = k_ref[pl.ds(k0row, tk), :]
            s = jnp.concatenate([_dot_nt(qa, kb[:, :hd]), _dot_nt(qb, kb[:, hd:])], axis=0)
            if masked:
                row = lax.broadcasted_iota(jnp.int32, (tq, tk), 0) + q0row
                col = lax.broadcasted_iota(jnp.int32, (tq, tk), 1) + k0row
                keep = col <= row
                keep = jnp.concatenate([keep, keep], axis=0)
                s = jnp.where(keep, s, NEG_BIG)
            m_prev = m_sc[...]
            m_new = jnp.maximum(m_prev, jnp.max(s, axis=-1, keepdims=True))
            alpha = jnp.exp2(m_prev - m_new)
            p = jnp.exp2(s - m_new).astype(BF16)
            pv = _dot(p, vext_sc[pl.ds(k0row, tk), :])
            acc_sc[...] = alpha * acc_sc[...] + pv
            m_sc[...] = m_new

        def body(j, c):
            kv_step(j, False)
            return c

        lax.fori_loop(0, qi * kv_per_q, body, 0)
        for d in range(kv_per_q):
            kv_step(qi * kv_per_q + d, True)

        acc = acc_sc[...]
        o0 = acc[:tq, :DA_V_DIM] / acc[:tq, DA_V_DIM:DA_V_DIM + 1]
        o1 = acc[tq:, :DA_V_DIM] / acc[tq:, DA_V_DIM:DA_V_DIM + 1]
        o = o0 - lam * o1
        y = _rms(o, sw_ref[...]) * (1.0 - lambda_init)
        o_ref[pl.ds(q0row, tq), :] = y.astype(BF16)
        return carry

    lax.fori_loop(0, nq, q_block, 0)


def _attention(proj, lam_p, subln_w, batch, seq, lambda_init):
    n = proj.shape[0]
    kern = functools.partial(_attn_kernel, seq=seq, lambda_init=lambda_init)
    hb = PROJ_TN // LANES
    return pl.pallas_call(
        kern,
        grid=(batch, DA_HEADS),
        in_specs=[
            pl.BlockSpec((4, DA_HEAD_DIM), lambda b, h: (0, 0)),
            pl.BlockSpec((1, DA_V_DIM), lambda b, h: (0, 0)),
            pl.BlockSpec((seq, LANES), lambda b, h: (b, SEG_Q * hb + h)),
            pl.BlockSpec((seq, LANES), lambda b, h: (b, SEG_K * hb + h)),
            pl.BlockSpec((seq, LANES), lambda b, h: (b, SEG_V * hb + h)),
        ],
        out_specs=pl.BlockSpec((seq, LANES), lambda b, h: (b, h)),
        out_shape=jax.ShapeDtypeStruct((n, DA_HEADS * DA_V_DIM), BF16),
        scratch_shapes=[
            pltpu.VMEM((seq, 2 * LANES), BF16),
            pltpu.VMEM((2 * ATT_TQ, 2 * LANES), F32),
            pltpu.VMEM((2 * ATT_TQ, 1), F32),
        ],
        compiler_params=pltpu.CompilerParams(
            dimension_semantics=("arbitrary", "arbitrary"),
            vmem_limit_bytes=VMEM_LIMIT_BYTES),
        name="diff_attention",
    )(lam_p, subln_w, proj, proj, proj)


def _mlstm_kernel(qk_ref, v_ref, so_ref, g_ref, cw_ref, cb_ref, gb_ref, nw_ref, out_ref,
                  xbuf, cext_sc, m_sc):
    L = ML_CHUNK
    dk, dv = ML_QK_DIM, ML_V_DIM
    qkw = ML_HEADS * dk
    c = pl.program_id(1)

    @pl.when(c == 0)
    def _():
        xbuf[0:SUBLANES, :] = jnp.zeros((SUBLANES, 2 * qkw), F32)
        cext_sc[...] = jnp.zeros(cext_sc.shape, F32)
        m_sc[...] = jnp.zeros(m_sc.shape, F32)

    xbuf[SUBLANES:SUBLANES + L, :] = qk_ref[...].astype(F32)
    conv = cb_ref[...]
    for j in range(CONV_WIDTH):
        off = SUBLANES - (CONV_WIDTH - 1) + j
        conv = conv + cw_ref[j:j + 1, :] * xbuf[off:off + L, :]
    xbuf[0:SUBLANES, :] = xbuf[L:L + SUBLANES, :]
    qkc = conv * _sigmoid(conv)
    q_all = (qkc[:, :qkw] * (dk ** -0.5)).astype(BF16)
    k_all = qkc[:, qkw:]

    g = g_ref[...] + gb_ref[...]
    logf = jnp.minimum(g, 0.0) - jnp.log(1.0 + jnp.exp(-jnp.abs(g)))
    row = lax.broadcasted_iota(jnp.int32, (L, L), 0)
    col = lax.broadcasted_iota(jnp.int32, (L, L), 1)
    tri = col <= row
    tri_b = tri.astype(BF16)
    hi = logf.astype(BF16)
    r1 = logf - hi.astype(F32)
    mid = r1.astype(BF16)
    lo = (r1 - mid.astype(F32)).astype(BF16)
    bcs = _dot(tri_b, hi) + _dot(tri_b, mid) + _dot(tri_b, lo)
    g_t = g.T
    b_t = bcs.T
    ones_col = (lax.broadcasted_iota(jnp.int32, (L, LANES), 1) == 0).astype(BF16)

    for h in range(ML_HEADS):
        b_col = bcs[:, ML_HEADS + h:ML_HEADS + h + 1]
        i_col = g[:, h:h + 1]
        a_row = g_t[h:h + 1, :] - b_t[ML_HEADS + h:ML_HEADS + h + 1, :]
        m_prev = m_sc[h:h + 1, 0:1]
        dmat = jnp.where(tri, b_col + a_row, NEG_BIG)
        m_inter = b_col + m_prev
        m_t = jnp.maximum(jnp.max(dmat, axis=-1, keepdims=True), m_inter)
        dexp = jnp.exp(dmat - m_t)
        inter = jnp.exp(m_inter - m_t)
        qh = q_all[:, h * dk:(h + 1) * dk]
        kh = k_all[:, h * dk:(h + 1) * dk]
        vh = v_ref[:, h * dv:(h + 1) * dv]
        vext = jnp.concatenate([vh, ones_col], axis=1)
        s = (_dot_nt(qh, kh.astype(BF16)) * dexp).astype(BF16)
        cext = cext_sc[h]
        hext = _dot(s, vext) + inter * _dot(qh, cext.astype(BF16))
        num = hext[:, :dv]
        den = hext[:, dv:dv + 1]
        hm = num / jnp.maximum(jnp.abs(den), jnp.exp(-m_t))
        b_last = b_col[L - 1:L, :]
        w_log = b_last - b_col + i_col
        m_new = jnp.maximum(b_last + m_prev, jnp.max(w_log, axis=0, keepdims=True))
        decay = jnp.exp(b_last + m_prev - m_new)
        ws = jnp.exp(w_log - m_new)
        kw = (kh * ws).astype(BF16)
        cext_sc[h] = decay * cext + _dot_tn(kw, vext)
        m_sc[h:h + 1, :] = jnp.broadcast_to(m_new, (1, LANES))
        y = _rms(hm, nw_ref[:, h * dv:(h + 1) * dv])
        y = so_ref[:, h * dv:(h + 1) * dv].astype(F32) * y
        out_ref[:, h * dv:(h + 1) * dv] = y.astype(BF16)


def _mlstm(proj, gates, conv_w, conv_b, gate_b, norm_w, batch, seq):
    n = proj.shape[0]
    L = ML_CHUNK
    nc = seq // L
    w = ML_HEADS * ML_V_DIM
    return pl.pallas_call(
        _mlstm_kernel,
        grid=(batch, nc),
        in_specs=[
            pl.BlockSpec((L, PROJ_TN), lambda b, c: (b * nc + c, SEG_MQK)),
            pl.BlockSpec((L, PROJ_TN), lambda b, c: (b * nc + c, SEG_MV)),
            pl.BlockSpec((L, PROJ_TN), lambda b, c: (b * nc + c, SEG_MO)),
            pl.BlockSpec((L, LANES), lambda b, c: (b * nc + c, 0)),
            pl.BlockSpec((CONV_WIDTH, 2 * ML_HEADS * ML_QK_DIM), lambda b, c: (0, 0)),
            pl.BlockSpec((1, 2 * ML_HEADS * ML_QK_DIM), lambda b, c: (0, 0)),
            pl.BlockSpec((1, LANES), lambda b, c: (0, 0)),
            pl.BlockSpec((1, w), lambda b, c: (0, 0)),
        ],
        out_specs=pl.BlockSpec((L, w), lambda b, c: (b * nc + c, 0)),
        out_shape=jax.ShapeDtypeStruct((n, w), BF16),
        scratch_shapes=[
            pltpu.VMEM((L + SUBLANES, 2 * ML_HEADS * ML_QK_DIM), F32),
            pltpu.VMEM((ML_HEADS, ML_QK_DIM, 2 * LANES), F32),
            pltpu.VMEM((ML_HEADS, LANES), F32),
        ],
        compiler_params=pltpu.CompilerParams(
            dimension_semantics=("arbitrary", "arbitrary"),
            vmem_limit_bytes=VMEM_LIMIT_BYTES),
        name="mlstm",
    )(proj, proj, proj, gates, conv_w, conv_b, gate_b, norm_w)


def _post_kernel(x_ref, oa_ref, ob_ref, g0_ref, g1_ref, wa_ref, wb_ref, wo_ref, nf_ref,
                 w1_ref, w2_ref, fw_ref, out_ref, *, final_norm):
    ya = _dot(oa_ref[...], wa_ref[...])
    yb = _dot(ob_ref[...], wb_ref[...])
    merged = g0_ref[...].astype(F32) * ya + g1_ref[...].astype(F32) * yb
    h = x_ref[...] + _dot(merged.astype(BF16), wo_ref[...])
    hn = _rms(h, nf_ref[...]).astype(BF16)
    acc = h
    for c0 in range(0, D_FF, FF_CHUNK):
        u = jnp.maximum(_dot(hn, w1_ref[:, c0:c0 + FF_CHUNK]), 0.0)
        acc = acc + _dot((u * u).astype(BF16), w2_ref[c0:c0 + FF_CHUNK, :])
    if final_norm:
        acc = _rms(acc, fw_ref[...])
    out_ref[...] = acc


def _post(x2, oa, ob, proj, wa, wb, wo, nf, w1, w2, fw, final_norm):
    n = x2.shape[0]
    tm = POST_TM
    const = lambda i: (0, 0)
    single = pl.Buffered(1)
    kern = functools.partial(_post_kernel, final_norm=final_norm)
    return pl.pallas_call(
        kern,
        grid=(n // tm,),
        in_specs=[
            pl.BlockSpec((tm, D_MODEL), lambda i: (i, 0)),
            pl.BlockSpec((tm, D_MODEL), lambda i: (i, 0)),
            pl.BlockSpec((tm, D_MODEL), lambda i: (i, 0)),
            pl.BlockSpec((tm, PROJ_TN), lambda i: (i, SEG_G0)),
            pl.BlockSpec((tm, PROJ_TN), lambda i: (i, SEG_G1)),
            pl.BlockSpec((D_MODEL, D_MODEL), const, pipeline_mode=single),
            pl.BlockSpec((D_MODEL, D_MODEL), const, pipeline_mode=single),
            pl.BlockSpec((D_MODEL, D_MODEL), const, pipeline_mode=single),
            pl.BlockSpec((1, D_MODEL), const),
            pl.BlockSpec((D_MODEL, D_FF), const, pipeline_mode=single),
            pl.BlockSpec((D_FF, D_MODEL), const, pipeline_mode=single),
            pl.BlockSpec((1, D_MODEL), const),
        ],
        out_specs=pl.BlockSpec((tm, D_MODEL), lambda i: (i, 0)),
        out_shape=jax.ShapeDtypeStruct((n, D_MODEL), F32),
        compiler_params=pltpu.CompilerParams(
            dimension_semantics=("arbitrary",),
            vmem_limit_bytes=VMEM_LIMIT_BYTES),
        name="post_mixer",
    )(x2, oa, ob, proj, proj, wa, wb, wo, nf, w1, w2, fw)


def kernel(x, positions, norm_mix_w, w_in, ml_gate_b, conv_w, conv_b, da_lambda, da_subln_w,
           ml_norm_w, w_proj_a, w_proj_b, w_out, norm_ffn_w, w_ff1, w_ff2, final_norm_w):
    batch, seq, _ = x.shape
    n = batch * seq
    depth = w_in.shape[0]
    assert seq % ATT_TQ == 0 and seq % ML_CHUNK == 0 and n % PROJ_TM == 0 and n % POST_TM == 0
    assert ATT_TQ % ATT_TK == 0

    da_w = DA_HEADS * 2 * DA_HEAD_DIM
    ml_qk = ML_HEADS * ML_QK_DIM
    ml_v = ML_HEADS * ML_V_DIM
    o_mq = 3 * da_w
    o_mv = o_mq + 2 * ml_qk
    o_gi = o_mv + ml_v
    o_mo = o_gi + 2 * ML_HEADS
    o_gate = o_mo + ml_v

    pos2 = positions.reshape(n, 1)
    inv = ROPE_THETA ** (-jnp.arange(0, ROPE_DIM, 2, dtype=F32) / ROPE_DIM)
    invf = jnp.tile(inv, LANES // (ROPE_DIM // 2)).reshape(1, LANES)

    h = x.reshape(n, D_MODEL)
    for l in range(depth):
        lambda_init = 0.8 - 0.6 * math.exp(-0.3 * l)
        w = w_in[l]
        wp = jnp.concatenate([w[:, :o_gi], w[:, o_mo:]], axis=1).astype(BF16)
        wg = jnp.pad(w[:, o_gi:o_mo], ((0, 0), (0, LANES - 2 * ML_HEADS))).astype(BF16)
        gate_b = jnp.pad(ml_gate_b[l], (0, LANES - 2 * ML_HEADS)).reshape(1, LANES)

        proj, gates = _inproj(h, pos2, norm_mix_w[l].reshape(1, D_MODEL), invf, wp, wg)
        oa = _attention(proj, da_lambda[l], da_subln_w[l].reshape(1, DA_V_DIM), batch, seq,
                        lambda_init)
        ob = _mlstm(proj, gates, conv_w[l], conv_b[l].reshape(1, -1), gate_b,
                    ml_norm_w[l].reshape(1, -1), batch, seq)
        h = _post(h, oa, ob, proj,
                  w_proj_a[l].astype(BF16), w_proj_b[l].astype(BF16), w_out[l].astype(BF16),
                  norm_ffn_w[l].reshape(1, D_MODEL), w_ff1[l].astype(BF16),
                  w_ff2[l].astype(BF16), final_norm_w.reshape(1, D_MODEL),
                  final_norm=(l == depth - 1))
    return h.reshape(batch, seq, D_MODEL)
```

```python
import functools
import math

import jax
import jax.numpy as jnp
from jax import lax
from jax.experimental import pallas as pl
from jax.experimental.pallas import tpu as pltpu

F32 = jnp.float32
BF16 = jnp.bfloat16

D_MODEL = 1024
DA_HEADS = 8
DA_HEAD_DIM = 64
DA_V_DIM = 128
ROPE_DIM = 16
ROPE_THETA = 500000.0
ML_HEADS = 8
ML_QK_DIM = 64
ML_V_DIM = 128
CONV_WIDTH = 4
D_FF = 4 * D_MODEL
EPS = 1e-6

LANES = 128
SUBLANES = 8
VMEM_LIMIT_BYTES = 56 * 1024 * 1024

PROJ_TM = 1024
PROJ_TN = 1024
ATT_TQ = 512
ATT_TK = 512
ML_CHUNK = 128
POST_TM = 512
FF_CHUNK = 1024

SEG_Q, SEG_K, SEG_V, SEG_MQK, SEG_MV, SEG_MO, SEG_G0, SEG_G1 = range(8)
N_SEG = 8

NEG_BIG = -1e30
LOG2E = 1.4426950408889634


def _dot(a, b):
    return jnp.dot(a, b, preferred_element_type=F32)


def _dot_nt(a, b):
    return lax.dot_general(a, b, (((1,), (1,)), ((), ())), preferred_element_type=F32)


def _dot_tn(a, b):
    return lax.dot_general(a, b, (((0,), (0,)), ((), ())), preferred_element_type=F32)


def _sigmoid(x):
    return 1.0 / (1.0 + jnp.exp(-x))


def _rms(x, w):
    return x * lax.rsqrt(jnp.mean(x * x, axis=-1, keepdims=True) + EPS) * w


def _inproj_kernel(x_ref, pos_ref, nw_ref, invf_ref, w_ref, wg_ref, out_ref, gates_ref,
                   xn_sc, cos_sc, sina_sc, sinb_sc):
    j = pl.program_id(1)

    @pl.when(j == 0)
    def _():
        xn = _rms(x_ref[...], nw_ref[...]).astype(BF16)
        xn_sc[...] = xn
        gates_ref[...] = _dot(xn, wg_ref[...])
        ang = pos_ref[...].astype(F32) * invf_ref[...]
        c = jnp.cos(ang)
        s = jnp.sin(ang)
        lane = lax.broadcasted_iota(jnp.int32, (1, LANES), 1) % DA_HEAD_DIM
        half = ROPE_DIM // 2
        cos_sc[...] = jnp.where(lane < ROPE_DIM, c, 1.0)
        sina_sc[...] = jnp.where(lane < half, -s, 0.0)
        sinb_sc[...] = jnp.where((lane >= half) & (lane < ROPE_DIM), s, 0.0)

    acc = _dot(xn_sc[...], w_ref[...])

    def rope_store(scale):
        cos = cos_sc[...]
        sina = sina_sc[...]
        sinb = sinb_sc[...]
        half = ROPE_DIM // 2
        for c0 in range(0, PROJ_TN, LANES):
            xc = acc[:, c0:c0 + LANES]
            r = (xc * cos + pltpu.roll(xc, LANES - half, 1) * sina
                 + pltpu.roll(xc, half, 1) * sinb)
            out_ref[:, c0:c0 + LANES] = (r * scale).astype(BF16)

    @pl.when(j == SEG_Q)
    def _():
        rope_store(DA_HEAD_DIM ** -0.5 * LOG2E)

    @pl.when(j == SEG_K)
    def _():
        rope_store(1.0)

    @pl.when((j == SEG_V) | (j == SEG_MQK) | (j == SEG_MV))
    def _():
        out_ref[...] = acc.astype(BF16)

    @pl.when(j >= SEG_MO)
    def _():
        out_ref[...] = _sigmoid(acc).astype(BF16)


def _inproj(x2, pos2, nw, invf, wp, wg):
    n = x2.shape[0]
    grid = (n // PROJ_TM, N_SEG)
    return pl.pallas_call(
        _inproj_kernel,
        grid=grid,
        in_specs=[
            pl.BlockSpec((PROJ_TM, D_MODEL), lambda i, j: (i, 0)),
            pl.BlockSpec((PROJ_TM, 1), lambda i, j: (i, 0)),
            pl.BlockSpec((1, D_MODEL), lambda i, j: (0, 0)),
            pl.BlockSpec((1, LANES), lambda i, j: (0, 0)),
            pl.BlockSpec((D_MODEL, PROJ_TN), lambda i, j: (0, j)),
            pl.BlockSpec((D_MODEL, LANES), lambda i, j: (0, 0)),
        ],
        out_specs=[
            pl.BlockSpec((PROJ_TM, PROJ_TN), lambda i, j: (i, j)),
            pl.BlockSpec((PROJ_TM, LANES), lambda i, j: (i, 0)),
        ],
        out_shape=[
            jax.ShapeDtypeStruct((n, N_SEG * PROJ_TN), BF16),
            jax.ShapeDtypeStruct((n, LANES), F32),
        ],
        scratch_shapes=[
            pltpu.VMEM((PROJ_TM, D_MODEL), BF16),
            pltpu.VMEM((PROJ_TM, LANES), F32),
            pltpu.VMEM((PROJ_TM, LANES), F32),
            pltpu.VMEM((PROJ_TM, LANES), F32),
        ],
        compiler_params=pltpu.CompilerParams(
            dimension_semantics=("arbitrary", "arbitrary"),
            vmem_limit_bytes=VMEM_LIMIT_BYTES),
        name="inproj",
    )(x2, pos2, nw, invf, wp, wg)


def _attn_kernel(lam_ref, sw_ref, q_ref, k_ref, v_ref, o_ref, vext_sc, sa_sc, sb_sc, acc_sc,
                 m_sc, *, seq, lambda_init):
    tq, tk = ATT_TQ, ATT_TK
    hd = DA_HEAD_DIM
    nq = seq // tq
    rows = 2 * tq
    nlt = tk // LANES

    vext_sc[:, :DA_V_DIM] = v_ref[...]
    ones_col = (lax.broadcasted_iota(jnp.int32, (seq, LANES), 1) == 0).astype(BF16)
    vext_sc[:, DA_V_DIM:] = ones_col

    lp = lam_ref[...]
    lam = (jnp.exp(jnp.sum(lp[0:1] * lp[1:2], axis=-1, keepdims=True))
           - jnp.exp(jnp.sum(lp[2:3] * lp[3:4], axis=-1, keepdims=True)) + lambda_init)

    def q_block(qi, carry):
        q0row = pl.multiple_of(qi * tq, tq)
        q = q_ref[pl.ds(q0row, tq), :]
        qa = q[:, :hd]
        qb = q[:, hd:]
        m_sc[...] = jnp.full((rows, LANES), NEG_BIG, F32)
        acc_sc[...] = jnp.zeros((rows, 2 * LANES), F32)

        def scores(dst, kblk, masked):
            k0row = pl.multiple_of(kblk * tk, tk)
            kb = k_ref[pl.ds(k0row, tk), :]
            s0 = _dot_nt(qa, kb[:, :hd])
            s1 = _dot_nt(qb, kb[:, hd:])
            if masked:
                keep = (lax.broadcasted_iota(jnp.int32, (tq, tk), 1)
                        <= lax.broadcasted_iota(jnp.int32, (tq, tk), 0))
                s0 = jnp.where(keep, s0, NEG_BIG)
                s1 = jnp.where(keep, s1, NEG_BIG)
            dst[0:tq, :] = s0
            dst[tq:rows, :] = s1

        def process(src, vblk):
            v0row = pl.multiple_of(vblk * tk, tk)
            m_prev = m_sc[...]
            smax = src[:, 0:LANES]
            for c in range(1, nlt):
                smax = jnp.maximum(smax, src[:, c * LANES:(c + 1) * LANES])
            m_new = jnp.maximum(m_prev, jnp.max(smax, axis=-1, keepdims=True))
            alpha = jnp.exp2(m_prev - m_new)
            p = jnp.concatenate(
                [jnp.exp2(src[:, c * LANES:(c + 1) * LANES] - m_new).astype(BF16)
                 for c in range(nlt)], axis=1)
            pv = _dot(p, vext_sc[pl.ds(v0row, tk), :])
            acc_sc[...] = jnp.concatenate([alpha, alpha], axis=1) * acc_sc[...] + pv
            m_sc[...] = m_new

        scores(sa_sc, qi, True)

        def pair(u, c):
            t0 = 2 * u
            scores(sb_sc, t0, False)
            process(sa_sc, jnp.where(u == 0, qi, t0 - 1))
            scores(sa_sc, t0 + 1, False)
            process(sb_sc, t0)
            return c

        lax.fori_loop(0, qi // 2, pair, 0)

        @pl.when(qi % 2 == 1)
        def _():
            scores(sb_sc, qi - 1, False)
            process(sa_sc, jnp.where(qi == 1, qi, qi - 2))
            process(sb_sc, qi - 1)

        @pl.when(qi % 2 == 0)
        def _():
            process(sa_sc, jnp.where(qi == 0, qi, qi - 1))

        acc = acc_sc[...]
        o = acc[:, :DA_V_DIM] / acc[:, DA_V_DIM:DA_V_DIM + 1]
        od = o[:tq] - lam * o[tq:]
        y = _rms(od, sw_ref[...]) * (1.0 - lambda_init)
        o_ref[pl.ds(q0row, tq), :] = y.astype(BF16)
        return carry

    lax.fori_loop(0, nq, q_block, 0)


def _attention(proj, lam_p, subln_w, batch, seq, lambda_init):
    n = proj.shape[0]
    kern = functools.partial(_attn_kernel, seq=seq, lambda_init=lambda_init)
    hb = PROJ_TN // LANES
    return pl.pallas_call(
        kern,
        grid=(batch, DA_HEADS),
        in_specs=[
            pl.BlockSpec((4, DA_HEAD_DIM), lambda b, h: (0, 0)),
            pl.BlockSpec((1, DA_V_DIM), lambda b, h: (0, 0)),
            pl.BlockSpec((seq, LANES), lambda b, h: (b, SEG_Q * hb + h)),
            pl.BlockSpec((seq, LANES), lambda b, h: (b, SEG_K * hb + h)),
            pl.BlockSpec((seq, LANES), lambda b, h: (b, SEG_V * hb + h)),
        ],
        out_specs=pl.BlockSpec((seq, LANES), lambda b, h: (b, h)),
        out_shape=jax.ShapeDtypeStruct((n, DA_HEADS * DA_V_DIM), BF16),
        scratch_shapes=[
            pltpu.VMEM((seq, 2 * LANES), BF16),
            pltpu.VMEM((2 * ATT_TQ, ATT_TK), F32),
            pltpu.VMEM((2 * ATT_TQ, ATT_TK), F32),
            pltpu.VMEM((2 * ATT_TQ, 2 * LANES), F32),
            pltpu.VMEM((2 * ATT_TQ, LANES), F32),
        ],
        compiler_params=pltpu.CompilerParams(
            dimension_semantics=("arbitrary", "arbitrary"),
            vmem_limit_bytes=VMEM_LIMIT_BYTES),
        name="diff_attention",
    )(lam_p, subln_w, proj, proj, proj)


def _mlstm_kernel(qk_ref, v_ref, so_ref, g_ref, cw_ref, cb_ref, gb_ref, nw_ref, out_ref,
                  xbuf, cext_sc, m_sc):
    L = ML_CHUNK
    dk, dv = ML_QK_DIM, ML_V_DIM
    qkw = ML_HEADS * dk
    c = pl.program_id(1)

    @pl.when(c == 0)
    def _():
        xbuf[0:SUBLANES, :] = jnp.zeros((SUBLANES, 2 * qkw), F32)
        cext_sc[...] = jnp.zeros(cext_sc.shape, F32)
        m_sc[...] = jnp.zeros(m_sc.shape, F32)

    xbuf[SUBLANES:SUBLANES + L, :] = qk_ref[...].astype(F32)
    conv = cb_ref[...]
    for j in range(CONV_WIDTH):
        off = SUBLANES - (CONV_WIDTH - 1) + j
        conv = conv + cw_ref[j:j + 1, :] * xbuf[off:off + L, :]
    xbuf[0:SUBLANES, :] = xbuf[L:L + SUBLANES, :]
    qkc = conv * _sigmoid(conv)
    q_all = (qkc[:, :qkw] * (dk ** -0.5)).astype(BF16)
    k_all = qkc[:, qkw:]

    g = g_ref[...] + gb_ref[...]
    logf = jnp.minimum(g, 0.0) - jnp.log(1.0 + jnp.exp(-jnp.abs(g)))
    row = lax.broadcasted_iota(jnp.int32, (L, L), 0)
    col = lax.broadcasted_iota(jnp.int32, (L, L), 1)
    tri = col <= row
    tri_b = tri.astype(BF16)
    hi = logf.astype(BF16)
    r1 = logf - hi.astype(F32)
    mid = r1.astype(BF16)
    lo = (r1 - mid.astype(F32)).astype(BF16)
    bcs = _dot(tri_b, hi) + _dot(tri_b, mid) + _dot(tri_b, lo)
    g_t = g.T
    b_t = bcs.T
    ones_col = (lax.broadcasted_iota(jnp.int32, (L, LANES), 1) == 0).astype(BF16)

    for h in range(ML_HEADS):
        b_col = bcs[:, ML_HEADS + h:ML_HEADS + h + 1]
        i_col = g[:, h:h + 1]
        a_row = g_t[h:h + 1, :] - b_t[ML_HEADS + h:ML_HEADS + h + 1, :]
        m_prev = m_sc[h:h + 1, 0:1]
        dmat = jnp.where(tri, b_col + a_row, NEG_BIG)
        m_inter = b_col + m_prev
        m_t = jnp.maximum(jnp.max(dmat, axis=-1, keepdims=True), m_inter)
        dexp = jnp.exp(dmat - m_t)
        inter = jnp.exp(m_inter - m_t)
        qh = q_all[:, h * dk:(h + 1) * dk]
        kh = k_all[:, h * dk:(h + 1) * dk]
        vh = v_ref[:, h * dv:(h + 1) * dv]
        vext = jnp.concatenate([vh, ones_col], axis=1)
        s = (_dot_nt(qh, kh.astype(BF16)) * dexp).astype(BF16)
        cext = cext_sc[h]
        hext = _dot(s, vext) + inter * _dot(qh, cext.astype(BF16))
        num = hext[:, :dv]
        den = hext[:, dv:dv + 1]
        hm = num / jnp.maximum(jnp.abs(den), jnp.exp(-m_t))
        b_last = b_col[L - 1:L, :]
        w_log = b_last - b_col + i_col
        m_new = jnp.maximum(b_last + m_prev, jnp.max(w_log, axis=0, keepdims=True))
        decay = jnp.exp(b_last + m_prev - m_new)
        ws = jnp.exp(w_log - m_new)
        kw = (kh * ws).astype(BF16)
        cext_sc[h] = decay * cext + _dot_tn(kw, vext)
        m_sc[h:h + 1, :] = jnp.broadcast_to(m_new, (1, LANES))
        y = _rms(hm, nw_ref[:, h * dv:(h + 1) * dv])
        y = so_ref[:, h * dv:(h + 1) * dv].astype(F32) * y
        out_ref[:, h * dv:(h + 1) * dv] = y.astype(BF16)


def _mlstm(proj, gates, conv_w, conv_b, gate_b, norm_w, batch, seq):
    n = proj.shape[0]
    L = ML_CHUNK
    nc = seq // L
    w = ML_HEADS * ML_V_DIM
    return pl.pallas_call(
        _mlstm_kernel,
        grid=(batch, nc),
        in_specs=[
            pl.BlockSpec((L, PROJ_TN), lambda b, c: (b * nc + c, SEG_MQK)),
            pl.BlockSpec((L, PROJ_TN), lambda b, c: (b * nc + c, SEG_MV)),
            pl.BlockSpec((L, PROJ_TN), lambda b, c: (b * nc + c, SEG_MO)),
            pl.BlockSpec((L, LANES), lambda b, c: (b * nc + c, 0)),
            pl.BlockSpec((CONV_WIDTH, 2 * ML_HEADS * ML_QK_DIM), lambda b, c: (0, 0)),
            pl.BlockSpec((1, 2 * ML_HEADS * ML_QK_DIM), lambda b, c: (0, 0)),
            pl.BlockSpec((1, LANES), lambda b, c: (0, 0)),
            pl.BlockSpec((1, w), lambda b, c: (0, 0)),
        ],
        out_specs=pl.BlockSpec((L, w), lambda b, c: (b * nc + c, 0)),
        out_shape=jax.ShapeDtypeStruct((n, w), BF16),
        scratch_shapes=[
            pltpu.VMEM((L + SUBLANES, 2 * ML_HEADS * ML_QK_DIM), F32),
            pltpu.VMEM((ML_HEADS, ML_QK_DIM, 2 * LANES), F32),
            pltpu.VMEM((ML_HEADS, LANES), F32),
        ],
        compiler_params=pltpu.CompilerParams(
            dimension_semantics=("arbitrary", "arbitrary"),
            vmem_limit_bytes=VMEM_LIMIT_BYTES),
        name="mlstm",
    )(proj, proj, proj, gates, conv_w, conv_b, gate_b, norm_w)


def _post_kernel(x_ref, oa_ref, ob_ref, g0_ref, g1_ref, wa_ref, wb_ref, wo_ref, nf_ref,
                 w1_ref, w2_ref, fw_ref, out_ref, *, final_norm):
    ya = _dot(oa_ref[...], wa_ref[...])
    yb = _dot(ob_ref[...], wb_ref[...])
    merged = g0_ref[...].astype(F32) * ya + g1_ref[...].astype(F32) * yb
    h = x_ref[...] + _dot(merged.astype(BF16), wo_ref[...])
    hn = _rms(h, nf_ref[...]).astype(BF16)
    acc = h
    for c0 in range(0, D_FF, FF_CHUNK):
        u = jnp.maximum(_dot(hn, w1_ref[:, c0:c0 + FF_CHUNK]), 0.0)
        acc = acc + _dot((u * u).astype(BF16), w2_ref[c0:c0 + FF_CHUNK, :])
    if final_norm:
        acc = _rms(acc, fw_ref[...])
    out_ref[...] = acc


def _post(x2, oa, ob, proj, wa, wb, wo, nf, w1, w2, fw, final_norm):
    n = x2.shape[0]
    tm = POST_TM
    const = lambda i: (0, 0)
    single = pl.Buffered(1)
    kern = functools.partial(_post_kernel, final_norm=final_norm)
    return pl.pallas_call(
        kern,
        grid=(n // tm,),
        in_specs=[
            pl.BlockSpec((tm, D_MODEL), lambda i: (i, 0)),
            pl.BlockSpec((tm, D_MODEL), lambda i: (i, 0)),
            pl.BlockSpec((tm, D_MODEL), lambda i: (i, 0)),
            pl.BlockSpec((tm, PROJ_TN), lambda i: (i, SEG_G0)),
            pl.BlockSpec((tm, PROJ_TN), lambda i: (i, SEG_G1)),
            pl.BlockSpec((D_MODEL, D_MODEL), const, pipeline_mode=single),
            pl.BlockSpec((D_MODEL, D_MODEL), const, pipeline_mode=single),
            pl.BlockSpec((D_MODEL, D_MODEL), const, pipeline_mode=single),
            pl.BlockSpec((1, D_MODEL), const),
            pl.BlockSpec((D_MODEL, D_FF), const, pipeline_mode=single),
            pl.BlockSpec((D_FF, D_MODEL), const, pipeline_mode=single),
            pl.BlockSpec((1, D_MODEL), const),
        ],
        out_specs=pl.BlockSpec((tm, D_MODEL), lambda i: (i, 0)),
        out_shape=jax.ShapeDtypeStruct((n, D_MODEL), F32),
        compiler_params=pltpu.CompilerParams(
            dimension_semantics=("arbitrary",),
            vmem_limit_bytes=VMEM_LIMIT_BYTES),
        name="post_mixer",
    )(x2, oa, ob, proj, proj, wa, wb, wo, nf, w1, w2, fw)


def kernel(x, positions, norm_mix_w, w_in, ml_gate_b, conv_w, conv_b, da_lambda, da_subln_w,
           ml_norm_w, w_proj_a, w_proj_b, w_out, norm_ffn_w, w_ff1, w_ff2, final_norm_w):
    batch, seq, _ = x.shape
    n = batch * seq
    depth = w_in.shape[0]
    assert seq % ATT_TQ == 0 and seq % ML_CHUNK == 0 and n % PROJ_TM == 0 and n % POST_TM == 0
    assert ATT_TQ == ATT_TK

    da_w = DA_HEADS * 2 * DA_HEAD_DIM
    ml_qk = ML_HEADS * ML_QK_DIM
    ml_v = ML_HEADS * ML_V_DIM
    o_mq = 3 * da_w
    o_mv = o_mq + 2 * ml_qk
    o_gi = o_mv + ml_v
    o_mo = o_gi + 2 * ML_HEADS
    o_gate = o_mo + ml_v

    pos2 = positions.reshape(n, 1)
    inv = ROPE_THETA ** (-jnp.arange(0, ROPE_DIM, 2, dtype=F32) / ROPE_DIM)
    invf = jnp.tile(inv, LANES // (ROPE_DIM // 2)).reshape(1, LANES)

    h = x.reshape(n, D_MODEL)
    for l in range(depth):
        lambda_init = 0.8 - 0.6 * math.exp(-0.3 * l)
        w = w_in[l]
        wp = jnp.concatenate([w[:, :o_gi], w[:, o_mo:]], axis=1).astype(BF16)
        wg = jnp.pad(w[:, o_gi:o_mo], ((0, 0), (0, LANES - 2 * ML_HEADS))).astype(BF16)
        gate_b = jnp.pad(ml_gate_b[l], (0, LANES - 2 * ML_HEADS)).reshape(1, LANES)

        proj, gates = _inproj(h, pos2, norm_mix_w[l].reshape(1, D_MODEL), invf, wp, wg)
        oa = _attention(proj, da_lambda[l], da_subln_w[l].reshape(1, DA_V_DIM), batch, seq,
                        lambda_init)
        ob = _mlstm(proj, gates, conv_w[l], conv_b[l].reshape(1, -1), gate_b,
                    ml_norm_w[l].reshape(1, -1), batch, seq)
        h = _post(h, oa, ob, proj,
                  w_proj_a[l].astype(BF16), w_proj_b[l].astype(BF16), w_out[l].astype(BF16),
                  norm_ffn_w[l].reshape(1, D_MODEL), w_ff1[l].astype(BF16),
                  w_ff2[l].astype(BF16), final_norm_w.reshape(1, D_MODEL),
                  final_norm=(l == depth - 1))
    return h.reshape(batch, seq, D_MODEL)
```

```python
import functools
import math

import jax
import jax.numpy as jnp
from jax import lax
from jax.experimental import pallas as pl
from jax.experimental.pallas import tpu as pltpu

F32 = jnp.float32
BF16 = jnp.bfloat16

D_MODEL = 1024
DA_HEADS = 8
DA_HEAD_DIM = 64
DA_V_DIM = 128
ROPE_DIM = 16
ROPE_THETA = 500000.0
ML_HEADS = 8
ML_QK_DIM = 64
ML_V_DIM = 128
CONV_WIDTH = 4
D_FF = 4 * D_MODEL
EPS = 1e-6

LANES = 128
SUBLANES = 8
VMEM_LIMIT_BYTES = 56 * 1024 * 1024

PROJ_TM = 512
PROJ_TN = 1024
PROJ_CN = 256
ATT_TQ = 512
ATT_TK = 512
ML_CHUNK = 256
POST_TM = 512
FF_CHUNK = 1024

SEG_Q, SEG_K, SEG_V, SEG_MQK, SEG_MV, SEG_MO, SEG_G0, SEG_G1 = range(8)
N_SEG = 8

NEG_BIG = -1e30
LOG2E = 1.4426950408889634


def _dot(a, b):
    return jnp.dot(a, b, preferred_element_type=F32)


def _dot_nt(a, b):
    return lax.dot_general(a, b, (((1,), (1,)), ((), ())), preferred_element_type=F32)


def _dot_tn(a, b):
    return lax.dot_general(a, b, (((0,), (0,)), ((), ())), preferred_element_type=F32)


def _sigmoid(x):
    return 0.5 * jnp.tanh(0.5 * x) + 0.5


def _rms(x, w):
    return x * lax.rsqrt(jnp.mean(x * x, axis=-1, keepdims=True) + EPS) * w


def _inproj_kernel(x_ref, pos_ref, nw_ref, invf_ref, cw_ref, cb_ref, w_ref, wg_ref,
                   out_ref, gates_ref, cbuf, *, tiles_per_seq):
    tm = PROJ_TM
    cn = PROJ_CN
    i = pl.program_id(0)

    xn = _rms(x_ref[...], nw_ref[...]).astype(BF16)
    gates_ref[...] = _dot(xn, wg_ref[...])

    ang = pos_ref[...].astype(F32) * invf_ref[...]
    c = jnp.cos(ang)
    s = jnp.sin(ang)
    lane = lax.broadcasted_iota(jnp.int32, (1, LANES), 1) % DA_HEAD_DIM
    half = ROPE_DIM // 2
    cos = jnp.where(lane < ROPE_DIM, c, 1.0)
    sina = jnp.where(lane < half, -s, 0.0)
    sinb = jnp.where((lane >= half) & (lane < ROPE_DIM), s, 0.0)

    hist_rows = CONV_WIDTH - 1
    first = i % tiles_per_seq == 0

    @pl.when(first)
    def _():
        cbuf[0:SUBLANES, :] = jnp.zeros((SUBLANES, PROJ_TN), F32)

    @pl.when(jnp.logical_not(first))
    def _():
        cbuf[0:SUBLANES, :] = cbuf[tm:tm + SUBLANES, :]

    seg_order = (SEG_V, SEG_Q, SEG_MV, SEG_MQK, SEG_G0, SEG_K, SEG_MO, SEG_G1)
    for c_in_seg in range(0, PROJ_TN, cn):
        for seg in seg_order:
            c0 = seg * PROJ_TN + c_in_seg
            acc = _dot(xn, w_ref[:, c0:c0 + cn])
            if seg in (SEG_Q, SEG_K):
                scale = DA_HEAD_DIM ** -0.5 * LOG2E if seg == SEG_Q else 1.0
                for l0 in range(0, cn, LANES):
                    xc = acc[:, l0:l0 + LANES]
                    r = (xc * cos + pltpu.roll(xc, LANES - half, 1) * sina
                         + pltpu.roll(xc, half, 1) * sinb)
                    out_ref[:, c0 + l0:c0 + l0 + LANES] = (r * scale).astype(BF16)
            elif seg == SEG_MQK:
                m0 = c0 - SEG_MQK * PROJ_TN
                cbuf[SUBLANES:SUBLANES + tm, m0:m0 + cn] = acc
                conv = cb_ref[:, m0:m0 + cn]
                for j in range(CONV_WIDTH):
                    off = SUBLANES - hist_rows + j
                    conv = conv + (cw_ref[j:j + 1, m0:m0 + cn]
                                   * cbuf[off:off + tm, m0:m0 + cn])
                y = conv * _sigmoid(conv)
                if m0 < ML_HEADS * ML_QK_DIM:
                    y = y * (ML_QK_DIM ** -0.5)
                out_ref[:, c0:c0 + cn] = y.astype(BF16)
            elif seg in (SEG_V, SEG_MV):
                out_ref[:, c0:c0 + cn] = acc.astype(BF16)
            else:
                out_ref[:, c0:c0 + cn] = _sigmoid(acc).astype(BF16)


def _inproj(x2, pos2, nw, invf, conv_w, conv_b, wp, wg, seq):
    n = x2.shape[0]
    tm = PROJ_TM
    const = lambda i: (0, 0)
    single = pl.Buffered(1)
    kern = functools.partial(_inproj_kernel, tiles_per_seq=seq // tm)
    return pl.pallas_call(
        kern,
        grid=(n // tm,),
        in_specs=[
            pl.BlockSpec((tm, D_MODEL), lambda i: (i, 0)),
            pl.BlockSpec((tm, 1), lambda i: (i, 0)),
            pl.BlockSpec((1, D_MODEL), const),
            pl.BlockSpec((1, LANES), const),
            pl.BlockSpec((CONV_WIDTH, PROJ_TN), const),
            pl.BlockSpec((1, PROJ_TN), const),
            pl.BlockSpec((D_MODEL, N_SEG * PROJ_TN), const, pipeline_mode=single),
            pl.BlockSpec((D_MODEL, LANES), const, pipeline_mode=single),
        ],
        out_specs=[
            pl.BlockSpec((tm, N_SEG * PROJ_TN), lambda i: (i, 0)),
            pl.BlockSpec((tm, LANES), lambda i: (i, 0)),
        ],
        out_shape=[
            jax.ShapeDtypeStruct((n, N_SEG * PROJ_TN), BF16),
            jax.ShapeDtypeStruct((n, LANES), F32),
        ],
        scratch_shapes=[
            pltpu.VMEM((tm + SUBLANES, PROJ_TN), F32),
        ],
        compiler_params=pltpu.CompilerParams(
            dimension_semantics=("arbitrary",),
            vmem_limit_bytes=VMEM_LIMIT_BYTES),
        name="inproj",
    )(x2, pos2, nw, invf, conv_w, conv_b, wp, wg)


def _attn_kernel(lam_ref, sw_ref, q_ref, k_ref, v_ref, o_ref, vext_sc, sa_sc, sb_sc, acc_sc,
                 m_sc, *, seq, lambda_init):
    tq, tk = ATT_TQ, ATT_TK
    hd = DA_HEAD_DIM
    nq = seq // tq
    rows = 2 * tq
    nlt = tk // LANES

    vext_sc[:, :DA_V_DIM] = v_ref[...]
    ones_col = (lax.broadcasted_iota(jnp.int32, (seq, LANES), 1) == 0).astype(BF16)
    vext_sc[:, DA_V_DIM:] = ones_col

    lp = lam_ref[...]
    lam = (jnp.exp(jnp.sum(lp[0:1] * lp[1:2], axis=-1, keepdims=True))
           - jnp.exp(jnp.sum(lp[2:3] * lp[3:4], axis=-1, keepdims=True)) + lambda_init)

    def q_block(qi, carry):
        q0row = pl.multiple_of(qi * tq, tq)
        q = q_ref[pl.ds(q0row, tq), :]
        qa = q[:, :hd]
        qb = q[:, hd:]
        m_sc[...] = jnp.full((rows, LANES), NEG_BIG, F32)
        acc_sc[...] = jnp.zeros((rows, 2 * LANES), F32)

        def scores(dst, kblk, masked):
            k0row = pl.multiple_of(kblk * tk, tk)
            kb = k_ref[pl.ds(k0row, tk), :]
            s0 = _dot_nt(qa, kb[:, :hd])
            s1 = _dot_nt(qb, kb[:, hd:])
            if masked:
                keep = (lax.broadcasted_iota(jnp.int32, (tq, tk), 1)
                        <= lax.broadcasted_iota(jnp.int32, (tq, tk), 0))
                s0 = jnp.where(keep, s0, NEG_BIG)
                s1 = jnp.where(keep, s1, NEG_BIG)
            dst[0:tq, :] = s0
            dst[tq:rows, :] = s1

        def process(src, vblk):
            v0row = pl.multiple_of(vblk * tk, tk)
            m_prev = m_sc[...]
            smax = src[:, 0:LANES]
            for c in range(1, nlt):
                smax = jnp.maximum(smax, src[:, c * LANES:(c + 1) * LANES])
            m_new = jnp.maximum(m_prev, jnp.max(smax, axis=-1, keepdims=True))
            alpha = jnp.exp2(m_prev - m_new)
            p = jnp.concatenate(
                [jnp.exp2(src[:, c * LANES:(c + 1) * LANES] - m_new).astype(BF16)
                 for c in range(nlt)], axis=1)
            pv = _dot(p, vext_sc[pl.ds(v0row, tk), :])
            acc_sc[...] = jnp.concatenate([alpha, alpha], axis=1) * acc_sc[...] + pv
            m_sc[...] = m_new

        scores(sa_sc, qi, True)

        def pair(u, c):
            t0 = 2 * u
            scores(sb_sc, t0, False)
            process(sa_sc, jnp.where(u == 0, qi, t0 - 1))
            scores(sa_sc, t0 + 1, False)
            process(sb_sc, t0)
            return c

        lax.fori_loop(0, qi // 2, pair, 0)

        @pl.when(qi % 2 == 1)
        def _():
            scores(sb_sc, qi - 1, False)
            process(sa_sc, jnp.where(qi == 1, qi, qi - 2))
            process(sb_sc, qi - 1)

        @pl.when(qi % 2 == 0)
        def _():
            process(sa_sc, jnp.where(qi == 0, qi, qi - 1))

        acc = acc_sc[...]
        o = acc[:, :DA_V_DIM] / acc[:, DA_V_DIM:DA_V_DIM + 1]
        od = o[:tq] - lam * o[tq:]
        y = _rms(od, sw_ref[...]) * (1.0 - lambda_init)
        o_ref[pl.ds(q0row, tq), :] = y.astype(BF16)
        return carry

    lax.fori_loop(0, nq, q_block, 0)


def _attention(proj, lam_p, subln_w, batch, seq, lambda_init):
    n = proj.shape[0]
    kern = functools.partial(_attn_kernel, seq=seq, lambda_init=lambda_init)
    hb = PROJ_TN // LANES
    return pl.pallas_call(
        kern,
        grid=(batch, DA_HEADS),
        in_specs=[
            pl.BlockSpec((4, DA_HEAD_DIM), lambda b, h: (0, 0)),
            pl.BlockSpec((1, DA_V_DIM), lambda b, h: (0, 0)),
            pl.BlockSpec((seq, LANES), lambda b, h: (b, SEG_Q * hb + h)),
            pl.BlockSpec((seq, LANES), lambda b, h: (b, SEG_K * hb + h)),
            pl.BlockSpec((seq, LANES), lambda b, h: (b, SEG_V * hb + h)),
        ],
        out_specs=pl.BlockSpec((seq, LANES), lambda b, h: (b, h)),
        out_shape=jax.ShapeDtypeStruct((n, DA_HEADS * DA_V_DIM), BF16),
        scratch_shapes=[
            pltpu.VMEM((seq, 2 * LANES), BF16),
            pltpu.VMEM((2 * ATT_TQ, ATT_TK), F32),
            pltpu.VMEM((2 * ATT_TQ, ATT_TK), F32),
            pltpu.VMEM((2 * ATT_TQ, 2 * LANES), F32),
            pltpu.VMEM((2 * ATT_TQ, LANES), F32),
        ],
        compiler_params=pltpu.CompilerParams(
            dimension_semantics=("arbitrary", "arbitrary"),
            vmem_limit_bytes=VMEM_LIMIT_BYTES),
        name="diff_attention",
    )(lam_p, subln_w, proj, proj, proj)


def _mlstm_kernel(qk_ref, v_ref, so_ref, g_ref, gb_ref, nw_ref, out_ref, cext_sc, m_sc,
                  mask_sc):
    L = ML_CHUNK
    dk, dv = ML_QK_DIM, ML_V_DIM
    qkw = ML_HEADS * dk
    nlt = L // LANES
    c = pl.program_id(1)

    @pl.when(c == 0)
    def _():
        cext_sc[...] = jnp.zeros(cext_sc.shape, F32)
        m_sc[...] = jnp.zeros(m_sc.shape, F32)
        row = lax.broadcasted_iota(jnp.int32, (L, L), 0)
        col = lax.broadcasted_iota(jnp.int32, (L, L), 1)
        mask_sc[...] = jnp.where(col <= row, 0.0, NEG_BIG)

    g = g_ref[...] + gb_ref[...]
    logf = jnp.minimum(g, 0.0) - jnp.log(1.0 + jnp.exp(-jnp.abs(g)))
    tri_b = (mask_sc[...] == 0.0).astype(BF16)
    hi = logf.astype(BF16)
    r1 = logf - hi.astype(F32)
    mid = r1.astype(BF16)
    lo = (r1 - mid.astype(F32)).astype(BF16)
    bcs = _dot(tri_b, hi) + _dot(tri_b, mid) + _dot(tri_b, lo)
    b_al = pltpu.roll(bcs, LANES - ML_HEADS, 1)
    a = g - b_al
    m_prev = m_sc[0:1, :]
    rowi = lax.broadcasted_iota(jnp.int32, (L, LANES), 0)
    cm = a
    d = 1
    while d < L:
        cm = jnp.maximum(cm, jnp.where(rowi >= d, pltpu.roll(cm, d, 0), NEG_BIG))
        d *= 2
    u = jnp.maximum(cm, m_prev)
    mt = b_al + u
    b_last = b_al[L - 1:L, :]
    w_log = b_last + a
    m_new = jnp.maximum(b_last + m_prev, jnp.max(w_log, axis=0, keepdims=True))
    decay = jnp.exp(b_last + m_prev - m_new)
    ws = jnp.exp(w_log - m_new)
    m_sc[0:1, :] = m_new
    a_t = a.T
    ws_t = ws.T
    ones_blk = jnp.ones((L, LANES), BF16)

    for h in range(ML_HEADS):
        u_b = jnp.broadcast_to(u[:, h:h + 1], (L, LANES))
        mt_b = jnp.broadcast_to(mt[:, h:h + 1], (L, LANES))
        inter_b = jnp.exp(m_prev[:, h:h + 1] - u_b)
        floor_b = jnp.exp(-mt_b)
        dexp = jnp.concatenate(
            [jnp.exp(a_t[h:h + 1, t * LANES:(t + 1) * LANES] - u_b
                     + mask_sc[:, t * LANES:(t + 1) * LANES]) for t in range(nlt)], axis=1)
        qh = qk_ref[:, h * dk:(h + 1) * dk]
        kh_t = qk_ref[:, qkw + h * dk:qkw + (h + 1) * dk].T
        vext = jnp.concatenate([v_ref[:, h * dv:(h + 1) * dv], ones_blk], axis=1)
        s = (_dot(qh, kh_t) * dexp).astype(BF16)
        cext = cext_sc[h]
        hext = (_dot(s, vext)
                + jnp.concatenate([inter_b, inter_b], axis=1) * _dot(qh, cext.astype(BF16)))
        hm = hext[:, :dv] / jnp.maximum(jnp.abs(hext[:, dv:]), floor_b)
        y = _rms(hm, nw_ref[:, h * dv:(h + 1) * dv])
        y = so_ref[:, h * dv:(h + 1) * dv].astype(F32) * y
        out_ref[:, h * dv:(h + 1) * dv] = y.astype(BF16)
        kw_t = (kh_t.astype(F32) * ws_t[h:h + 1, :]).astype(BF16)
        cext_sc[h] = decay[:, h:h + 1] * cext + _dot(kw_t, vext)


def _mlstm(proj, gates, gate_b, norm_w, batch, seq):
    n = proj.shape[0]
    L = ML_CHUNK
    nc = seq // L
    w = ML_HEADS * ML_V_DIM
    return pl.pallas_call(
        _mlstm_kernel,
        grid=(batch, nc),
        in_specs=[
            pl.BlockSpec((L, PROJ_TN), lambda b, c: (b * nc + c, SEG_MQK)),
            pl.BlockSpec((L, PROJ_TN), lambda b, c: (b * nc + c, SEG_MV)),
            pl.BlockSpec((L, PROJ_TN), lambda b, c: (b * nc + c, SEG_MO)),
            pl.BlockSpec((L, LANES), lambda b, c: (b * nc + c, 0)),
            pl.BlockSpec((1, LANES), lambda b, c: (0, 0)),
            pl.BlockSpec((1, w), lambda b, c: (0, 0)),
        ],
        out_specs=pl.BlockSpec((L, w), lambda b, c: (b * nc + c, 0)),
        out_shape=jax.ShapeDtypeStruct((n, w), BF16),
        scratch_shapes=[
            pltpu.VMEM((ML_HEADS, ML_QK_DIM, 2 * LANES), F32),
            pltpu.VMEM((SUBLANES, LANES), F32),
            pltpu.VMEM((L, L), F32),
        ],
        compiler_params=pltpu.CompilerParams(
            dimension_semantics=("arbitrary", "arbitrary"),
            vmem_limit_bytes=VMEM_LIMIT_BYTES),
        name="mlstm",
    )(proj, proj, proj, gates, gate_b, norm_w)


def _post_kernel(x_ref, oa_ref, ob_ref, g0_ref, g1_ref, wa_ref, wb_ref, wo_ref, nf_ref,
                 w1_ref, w2_ref, fw_ref, out_ref, *, final_norm):
    ya = _dot(oa_ref[...], wa_ref[...])
    yb = _dot(ob_ref[...], wb_ref[...])
    merged = g0_ref[...].astype(F32) * ya + g1_ref[...].astype(F32) * yb
    h = x_ref[...] + _dot(merged.astype(BF16), wo_ref[...])
    hn = _rms(h, nf_ref[...]).astype(BF16)
    acc = h
    for c0 in range(0, D_FF, FF_CHUNK):
        u = jnp.maximum(_dot(hn, w1_ref[:, c0:c0 + FF_CHUNK]), 0.0)
        acc = acc + _dot((u * u).astype(BF16), w2_ref[c0:c0 + FF_CHUNK, :])
    if final_norm:
        acc = _rms(acc, fw_ref[...])
    out_ref[...] = acc


def _post(x2, oa, ob, proj, wa, wb, wo, nf, w1, w2, fw, final_norm):
    n = x2.shape[0]
    tm = POST_TM
    const = lambda i: (0, 0)
    single = pl.Buffered(1)
    kern = functools.partial(_post_kernel, final_norm=final_norm)
    return pl.pallas_call(
        kern,
        grid=(n // tm,),
        in_specs=[
            pl.BlockSpec((tm, D_MODEL), lambda i: (i, 0)),
            pl.BlockSpec((tm, D_MODEL), lambda i: (i, 0)),
            pl.BlockSpec((tm, D_MODEL), lambda i: (i, 0)),
            pl.BlockSpec((tm, PROJ_TN), lambda i: (i, SEG_G0)),
            pl.BlockSpec((tm, PROJ_TN), lambda i: (i, SEG_G1)),
            pl.BlockSpec((D_MODEL, D_MODEL), const, pipeline_mode=single),
            pl.BlockSpec((D_MODEL, D_MODEL), const, pipeline_mode=single),
            pl.BlockSpec((D_MODEL, D_MODEL), const, pipeline_mode=single),
            pl.BlockSpec((1, D_MODEL), const),
            pl.BlockSpec((D_MODEL, D_FF), const, pipeline_mode=single),
            pl.BlockSpec((D_FF, D_MODEL), const, pipeline_mode=single),
            pl.BlockSpec((1, D_MODEL), const),
        ],
        out_specs=pl.BlockSpec((tm, D_MODEL), lambda i: (i, 0)),
        out_shape=jax.ShapeDtypeStruct((n, D_MODEL), F32),
        compiler_params=pltpu.CompilerParams(
            dimension_semantics=("arbitrary",),
            vmem_limit_bytes=VMEM_LIMIT_BYTES),
        name="post_mixer",
    )(x2, oa, ob, proj, proj, wa, wb, wo, nf, w1, w2, fw)


def kernel(x, positions, norm_mix_w, w_in, ml_gate_b, conv_w, conv_b, da_lambda, da_subln_w,
           ml_norm_w, w_proj_a, w_proj_b, w_out, norm_ffn_w, w_ff1, w_ff2, final_norm_w):
    batch, seq, _ = x.shape
    n = batch * seq
    depth = w_in.shape[0]
    assert seq % ATT_TQ == 0 and seq % ML_CHUNK == 0 and seq % PROJ_TM == 0 and n % POST_TM == 0
    assert ATT_TQ == ATT_TK

    da_w = DA_HEADS * 2 * DA_HEAD_DIM
    ml_qk = ML_HEADS * ML_QK_DIM
    ml_v = ML_HEADS * ML_V_DIM
    o_mq = 3 * da_w
    o_mv = o_mq + 2 * ml_qk
    o_gi = o_mv + ml_v
    o_mo = o_gi + 2 * ML_HEADS
    o_gate = o_mo + ml_v

    pos2 = positions.reshape(n, 1)
    inv = ROPE_THETA ** (-jnp.arange(0, ROPE_DIM, 2, dtype=F32) / ROPE_DIM)
    invf = jnp.tile(inv, LANES // (ROPE_DIM // 2)).reshape(1, LANES)

    h = x.reshape(n, D_MODEL)
    for l in range(depth):
        lambda_init = 0.8 - 0.6 * math.exp(-0.3 * l)
        w = w_in[l]
        wp = jnp.concatenate([w[:, :o_gi], w[:, o_mo:]], axis=1).astype(BF16)
        wg = jnp.pad(w[:, o_gi:o_mo], ((0, 0), (0, LANES - 2 * ML_HEADS))).astype(BF16)
        gate_b = jnp.pad(ml_gate_b[l], (0, LANES - 2 * ML_HEADS)).reshape(1, LANES)

        proj, gates = _inproj(h, pos2, norm_mix_w[l].reshape(1, D_MODEL), invf, conv_w[l],
                              conv_b[l].reshape(1, -1), wp, wg, seq)
        oa = _attention(proj, da_lambda[l], da_subln_w[l].reshape(1, DA_V_DIM), batch, seq,
                        lambda_init)
        ob = _mlstm(proj, gates, gate_b, ml_norm_w[l].reshape(1, -1), batch, seq)
        h = _post(h, oa, ob, proj,
                  w_proj_a[l].astype(BF16), w_proj_b[l].astype(BF16), w_out[l].astype(BF16),
                  norm_ffn_w[l].reshape(1, D_MODEL), w_ff1[l].astype(BF16),
                  w_ff2[l].astype(BF16), final_norm_w.reshape(1, D_MODEL),
                  final_norm=(l == depth - 1))
    return h.reshape(batch, seq, D_MODEL)
```

```python
import functools
import math

import jax
import jax.numpy as jnp
from jax import lax
from jax.experimental import pallas as pl
from jax.experimental.pallas import tpu as pltpu

F32 = jnp.float32
BF16 = jnp.bfloat16

D_MODEL = 1024
DA_HEADS = 8
DA_HEAD_DIM = 64
DA_V_DIM = 128
ROPE_DIM = 16
ROPE_THETA = 500000.0
ML_HEADS = 8
ML_QK_DIM = 64
ML_V_DIM = 128
CONV_WIDTH = 4
D_FF = 4 * D_MODEL
EPS = 1e-6

LANES = 128
SUBLANES = 8
VMEM_LIMIT_BYTES = 56 * 1024 * 1024

PROJ_TM = 512
PROJ_TN = 1024
PROJ_CN = 256
ATT_T = 512
ML_CHUNK = 256
POST_TM = 512
FF_CHUNK = 1024

SEG_Q, SEG_K, SEG_V, SEG_MQK, SEG_MV, SEG_MO, SEG_G0, SEG_G1 = range(8)
N_SEG = 8

NEG_BIG = -1e30
LOG2E = 1.4426950408889634


def _dot(a, b):
    return jnp.dot(a, b, preferred_element_type=F32)


def _dot_nt(a, b):
    return lax.dot_general(a, b, (((1,), (1,)), ((), ())), preferred_element_type=F32)


def _dot_tn(a, b):
    return lax.dot_general(a, b, (((0,), (0,)), ((), ())), preferred_element_type=F32)


def _sigmoid(x):
    return 0.5 * jnp.tanh(0.5 * x) + 0.5


def _rms(x, w):
    return x * lax.rsqrt(jnp.mean(x * x, axis=-1, keepdims=True) + EPS) * w


def _inproj_kernel(x_ref, pos_ref, nw_ref, invf_ref, cw_ref, cb_ref, w_ref, wg_ref,
                   out_ref, gates_ref, cbuf, *, tiles_per_seq):
    tm = PROJ_TM
    cn = PROJ_CN
    i = pl.program_id(0)

    xn = _rms(x_ref[...], nw_ref[...]).astype(BF16)
    gates_ref[...] = _dot(xn, wg_ref[...])

    ang = pos_ref[...].astype(F32) * invf_ref[...]
    c = jnp.cos(ang)
    s = jnp.sin(ang)
    lane = lax.broadcasted_iota(jnp.int32, (1, LANES), 1) % DA_HEAD_DIM
    half = ROPE_DIM // 2
    cos = jnp.where(lane < ROPE_DIM, c, 1.0)
    sina = jnp.where(lane < half, -s, 0.0)
    sinb = jnp.where((lane >= half) & (lane < ROPE_DIM), s, 0.0)

    hist_rows = CONV_WIDTH - 1
    first = i % tiles_per_seq == 0

    @pl.when(first)
    def _():
        cbuf[0:SUBLANES, :] = jnp.zeros((SUBLANES, PROJ_TN), F32)

    @pl.when(jnp.logical_not(first))
    def _():
        cbuf[0:SUBLANES, :] = cbuf[tm:tm + SUBLANES, :]

    seg_order = (SEG_V, SEG_Q, SEG_MV, SEG_MQK, SEG_G0, SEG_K, SEG_MO, SEG_G1)
    for c_in_seg in range(0, PROJ_TN, cn):
        for seg in seg_order:
            c0 = seg * PROJ_TN + c_in_seg
            acc = _dot(xn, w_ref[:, c0:c0 + cn])
            if seg in (SEG_Q, SEG_K):
                scale = DA_HEAD_DIM ** -0.5 * LOG2E if seg == SEG_Q else 1.0
                for l0 in range(0, cn, LANES):
                    xc = acc[:, l0:l0 + LANES]
                    r = (xc * cos + pltpu.roll(xc, LANES - half, 1) * sina
                         + pltpu.roll(xc, half, 1) * sinb)
                    out_ref[:, c0 + l0:c0 + l0 + LANES] = (r * scale).astype(BF16)
            elif seg == SEG_MQK:
                m0 = c0 - SEG_MQK * PROJ_TN
                cbuf[SUBLANES:SUBLANES + tm, m0:m0 + cn] = acc
                conv = cb_ref[:, m0:m0 + cn]
                for j in range(CONV_WIDTH):
                    off = SUBLANES - hist_rows + j
                    conv = conv + (cw_ref[j:j + 1, m0:m0 + cn]
                                   * cbuf[off:off + tm, m0:m0 + cn])
                y = conv * _sigmoid(conv)
                if m0 < ML_HEADS * ML_QK_DIM:
                    y = y * (ML_QK_DIM ** -0.5)
                out_ref[:, c0:c0 + cn] = y.astype(BF16)
            elif seg in (SEG_V, SEG_MV):
                out_ref[:, c0:c0 + cn] = acc.astype(BF16)
            else:
                out_ref[:, c0:c0 + cn] = _sigmoid(acc).astype(BF16)


def _inproj(x2, pos2, nw, invf, conv_w, conv_b, wp, wg, seq):
    n = x2.shape[0]
    tm = PROJ_TM
    const = lambda i: (0, 0)
    single = pl.Buffered(1)
    kern = functools.partial(_inproj_kernel, tiles_per_seq=seq // tm)
    return pl.pallas_call(
        kern,
        grid=(n // tm,),
        in_specs=[
            pl.BlockSpec((tm, D_MODEL), lambda i: (i, 0)),
            pl.BlockSpec((tm, 1), lambda i: (i, 0)),
            pl.BlockSpec((1, D_MODEL), const),
            pl.BlockSpec((1, LANES), const),
            pl.BlockSpec((CONV_WIDTH, PROJ_TN), const),
            pl.BlockSpec((1, PROJ_TN), const),
            pl.BlockSpec((D_MODEL, N_SEG * PROJ_TN), const, pipeline_mode=single),
            pl.BlockSpec((D_MODEL, LANES), const, pipeline_mode=single),
        ],
        out_specs=[
            pl.BlockSpec((tm, N_SEG * PROJ_TN), lambda i: (i, 0)),
            pl.BlockSpec((tm, LANES), lambda i: (i, 0)),
        ],
        out_shape=[
            jax.ShapeDtypeStruct((n, N_SEG * PROJ_TN), BF16),
            jax.ShapeDtypeStruct((n, LANES), F32),
        ],
        scratch_shapes=[
            pltpu.VMEM((tm + SUBLANES, PROJ_TN), F32),
        ],
        compiler_params=pltpu.CompilerParams(
            dimension_semantics=("arbitrary",),
            vmem_limit_bytes=VMEM_LIMIT_BYTES),
        name="inproj",
    )(x2, pos2, nw, invf, conv_w, conv_b, wp, wg)


def _attn_kernel(lam_ref, sw_ref, q_ref, k_ref, v_ref, o_ref, vext_sc, sa_sc, sb_sc, pa_sc,
                 pb_sc, acc_sc, m_sc, *, seq, lambda_init):
    t = ATT_T
    hd = DA_HEAD_DIM
    dv = DA_V_DIM
    nq = seq // t
    rows = 2 * t
    nlt = t // LANES
    n_off = nq * (nq - 1) // 2
    assert n_off >= 2 and n_off % 2 == 0

    @pl.when((pl.program_id(0) == 0) & (pl.program_id(1) == 0))
    def _():
        vext_sc[:, dv:] = jnp.ones((seq, LANES), BF16)

    vext_sc[:, :dv] = v_ref[...]

    lp = lam_ref[...]
    lam = (jnp.exp(jnp.sum(lp[0:1] * lp[1:2], axis=-1, keepdims=True))
           - jnp.exp(jnp.sum(lp[2:3] * lp[3:4], axis=-1, keepdims=True)) + lambda_init)

    def blk(i):
        return pl.ds(i * t if isinstance(i, int) else pl.multiple_of(i * t, t), t)

    def scores(dst_s, dst_pm, qi, kblk, masked):
        q = q_ref[blk(qi), :]
        lane = lax.broadcasted_iota(jnp.int32, (t, LANES), 1)
        zero = jnp.zeros_like(q)
        qs = jnp.concatenate([jnp.where(lane < hd, q, zero), jnp.where(lane >= hd, q, zero)],
                             axis=0)
        kb = k_ref[blk(kblk), :]
        s = _dot_nt(qs, kb)
        if masked:
            keep = (lax.broadcasted_iota(jnp.int32, (t, t), 1)
                    <= lax.broadcasted_iota(jnp.int32, (t, t), 0))
            s = jnp.where(jnp.concatenate([keep, keep], axis=0), s, NEG_BIG)
        dst_s[...] = s
        pm = s[:, 0:LANES]
        for c in range(1, nlt):
            pm = jnp.maximum(pm, s[:, c * LANES:(c + 1) * LANES])
        dst_pm[...] = pm

    def process(src_s, src_pm, qi, vblk, first):
        mx = jnp.max(src_pm[...], axis=-1, keepdims=True)
        if first:
            m_new = jnp.broadcast_to(mx, (rows, LANES))
        else:
            m_prev = m_sc[qi]
            m_new = jnp.maximum(m_prev, mx)
            alpha = jnp.exp2(m_prev - m_new)
        p = jnp.concatenate(
            [jnp.exp2(src_s[:, c * LANES:(c + 1) * LANES] - m_new).astype(BF16)
             for c in range(nlt)], axis=1)
        pv = _dot(p, vext_sc[blk(vblk), :])
        if first:
            acc_sc[qi] = pv
        else:
            acc_sc[qi] = jnp.concatenate([alpha, alpha], axis=1) * acc_sc[qi] + pv
        m_sc[qi] = m_new

    bufs = ((sa_sc, pa_sc), (sb_sc, pb_sc))

    for qi in range(nq):
        bs, bp = bufs[qi % 2]
        scores(bs, bp, qi, qi, True)
        process(bs, bp, qi, qi, True)

    def advance(qi, j):
        wrap = j + 1 >= qi
        return jnp.where(wrap, qi + 1, qi), jnp.where(wrap, 0, j + 1)

    q0 = jnp.int32(1)
    j0 = jnp.int32(0)
    scores(sa_sc, pa_sc, q0, j0, False)

    def pair(u, carry):
        qa, ja = carry
        qb, jb = advance(qa, ja)
        scores(sb_sc, pb_sc, qb, jb, False)
        process(sa_sc, pa_sc, qa, ja, False)
        qc, jc = advance(qb, jb)
        scores(sa_sc, pa_sc, qc, jc, False)
        process(sb_sc, pb_sc, qb, jb, False)
        return qc, jc

    qa, ja = lax.fori_loop(0, n_off // 2 - 1, pair, (q0, j0))
    qb, jb = advance(qa, ja)
    scores(sb_sc, pb_sc, qb, jb, False)
    process(sa_sc, pa_sc, qa, ja, False)
    process(sb_sc, pb_sc, qb, jb, False)

    for qi in range(nq):
        acc = acc_sc[qi]
        o = acc[:, :dv] / acc[:, dv:]
        od = o[:t] - lam * o[t:]
        y = _rms(od, sw_ref[...]) * (1.0 - lambda_init)
        o_ref[qi * t:(qi + 1) * t, :] = y.astype(BF16)


def _attention(proj, lam_p, subln_w, batch, seq, lambda_init):
    n = proj.shape[0]
    kern = functools.partial(_attn_kernel, seq=seq, lambda_init=lambda_init)
    hb = PROJ_TN // LANES
    return pl.pallas_call(
        kern,
        grid=(batch, DA_HEADS),
        in_specs=[
            pl.BlockSpec((4, DA_HEAD_DIM), lambda b, h: (0, 0)),
            pl.BlockSpec((1, DA_V_DIM), lambda b, h: (0, 0)),
            pl.BlockSpec((seq, LANES), lambda b, h: (b, SEG_Q * hb + h)),
            pl.BlockSpec((seq, LANES), lambda b, h: (b, SEG_K * hb + h)),
            pl.BlockSpec((seq, LANES), lambda b, h: (b, SEG_V * hb + h)),
        ],
        out_specs=pl.BlockSpec((seq, LANES), lambda b, h: (b, h)),
        out_shape=jax.ShapeDtypeStruct((n, DA_HEADS * DA_V_DIM), BF16),
        scratch_shapes=[
            pltpu.VMEM((seq, 2 * LANES), BF16),
            pltpu.VMEM((2 * ATT_T, ATT_T), F32),
            pltpu.VMEM((2 * ATT_T, ATT_T), F32),
            pltpu.VMEM((2 * ATT_T, LANES), F32),
            pltpu.VMEM((2 * ATT_T, LANES), F32),
            pltpu.VMEM((seq // ATT_T, 2 * ATT_T, 2 * LANES), F32),
            pltpu.VMEM((seq // ATT_T, 2 * ATT_T, LANES), F32),
        ],
        compiler_params=pltpu.CompilerParams(
            dimension_semantics=("arbitrary", "arbitrary"),
            vmem_limit_bytes=VMEM_LIMIT_BYTES),
        name="diff_attention",
    )(lam_p, subln_w, proj, proj, proj)


def _mlstm_kernel(qk_ref, v_ref, so_ref, g_ref, gb_ref, nw_ref, out_ref, cext_sc, m_sc,
                  mask_sc):
    L = ML_CHUNK
    dk, dv = ML_QK_DIM, ML_V_DIM
    qkw = ML_HEADS * dk
    nlt = L // LANES
    c = pl.program_id(1)

    @pl.when(c == 0)
    def _():
        cext_sc[...] = jnp.zeros(cext_sc.shape, F32)
        m_sc[...] = jnp.zeros(m_sc.shape, F32)
        row = lax.broadcasted_iota(jnp.int32, (L, L), 0)
        col = lax.broadcasted_iota(jnp.int32, (L, L), 1)
        mask_sc[...] = jnp.where(col <= row, 0.0, NEG_BIG)

    g = g_ref[...] + gb_ref[...]
    logf = jnp.minimum(g, 0.0) - jnp.log(1.0 + jnp.exp(-jnp.abs(g)))
    tri_b = (mask_sc[...] == 0.0).astype(BF16)
    hi = logf.astype(BF16)
    r1 = logf - hi.astype(F32)
    mid = r1.astype(BF16)
    lo = (r1 - mid.astype(F32)).astype(BF16)
    bcs = _dot(tri_b, hi) + _dot(tri_b, mid) + _dot(tri_b, lo)
    b_al = pltpu.roll(bcs, LANES - ML_HEADS, 1)
    a = g - b_al
    m_prev = m_sc[0:1, :]
    rowi = lax.broadcasted_iota(jnp.int32, (L, LANES), 0)
    cm = a
    d = 1
    while d < L:
        cm = jnp.maximum(cm, jnp.where(rowi >= d, pltpu.roll(cm, d, 0), NEG_BIG))
        d *= 2
    u = jnp.maximum(cm, m_prev)
    mt = b_al + u
    b_last = b_al[L - 1:L, :]
    w_log = b_last + a
    m_new = jnp.maximum(b_last + m_prev, jnp.max(w_log, axis=0, keepdims=True))
    decay = jnp.exp(b_last + m_prev - m_new)
    ws = jnp.exp(w_log - m_new)
    m_sc[0:1, :] = m_new
    a_t = a.T
    ws_t = ws.T
    ones_blk = jnp.ones((L, LANES), BF16)

    for h in range(ML_HEADS):
        u_b = jnp.broadcast_to(u[:, h:h + 1], (L, LANES))
        mt_b = jnp.broadcast_to(mt[:, h:h + 1], (L, LANES))
        inter_b = jnp.exp(m_prev[:, h:h + 1] - u_b)
        floor_b = jnp.exp(-mt_b)
        dexp = jnp.concatenate(
            [jnp.exp(a_t[h:h + 1, t * LANES:(t + 1) * LANES] - u_b
                     + mask_sc[:, t * LANES:(t + 1) * LANES]) for t in range(nlt)], axis=1)
        qh = qk_ref[:, h * dk:(h + 1) * dk]
        kh_t = qk_ref[:, qkw + h * dk:qkw + (h + 1) * dk].T
        vext = jnp.concatenate([v_ref[:, h * dv:(h + 1) * dv], ones_blk], axis=1)
        s = (_dot(qh, kh_t) * dexp).astype(BF16)
        cext = cext_sc[h]
        hext = (_dot(s, vext)
                + jnp.concatenate([inter_b, inter_b], axis=1) * _dot(qh, cext.astype(BF16)))
        hm = hext[:, :dv] / jnp.maximum(jnp.abs(hext[:, dv:]), floor_b)
        y = _rms(hm, nw_ref[:, h * dv:(h + 1) * dv])
        y = so_ref[:, h * dv:(h + 1) * dv].astype(F32) * y
        out_ref[:, h * dv:(h + 1) * dv] = y.astype(BF16)
        kw_t = (kh_t.astype(F32) * ws_t[h:h + 1, :]).astype(BF16)
        cext_sc[h] = decay[:, h:h + 1] * cext + _dot(kw_t, vext)


def _mlstm(proj, gates, gate_b, norm_w, batch, seq):
    n = proj.shape[0]
    L = ML_CHUNK
    nc = seq // L
    w = ML_HEADS * ML_V_DIM
    return pl.pallas_call(
        _mlstm_kernel,
        grid=(batch, nc),
        in_specs=[
            pl.BlockSpec((L, PROJ_TN), lambda b, c: (b * nc + c, SEG_MQK)),
            pl.BlockSpec((L, PROJ_TN), lambda b, c: (b * nc + c, SEG_MV)),
            pl.BlockSpec((L, PROJ_TN), lambda b, c: (b * nc + c, SEG_MO)),
            pl.BlockSpec((L, LANES), lambda b, c: (b * nc + c, 0)),
            pl.BlockSpec((1, LANES), lambda b, c: (0, 0)),
            pl.BlockSpec((1, w), lambda b, c: (0, 0)),
        ],
        out_specs=pl.BlockSpec((L, w), lambda b, c: (b * nc + c, 0)),
        out_shape=jax.ShapeDtypeStruct((n, w), BF16),
        scratch_shapes=[
            pltpu.VMEM((ML_HEADS, ML_QK_DIM, 2 * LANES), F32),
            pltpu.VMEM((SUBLANES, LANES), F32),
            pltpu.VMEM((L, L), F32),
        ],
        compiler_params=pltpu.CompilerParams(
            dimension_semantics=("arbitrary", "arbitrary"),
            vmem_limit_bytes=VMEM_LIMIT_BYTES),
        name="mlstm",
    )(proj, proj, proj, gates, gate_b, norm_w)


def _post_kernel(x_ref, oa_ref, ob_ref, g0_ref, g1_ref, wa_ref, wb_ref, wo_ref, nf_ref,
                 w1_ref, w2_ref, fw_ref, out_ref, *, final_norm):
    ya = _dot(oa_ref[...], wa_ref[...])
    yb = _dot(ob_ref[...], wb_ref[...])
    merged = g0_ref[...].astype(F32) * ya + g1_ref[...].astype(F32) * yb
    h = x_ref[...] + _dot(merged.astype(BF16), wo_ref[...])
    hn = _rms(h, nf_ref[...]).astype(BF16)
    acc = h
    for c0 in range(0, D_FF, FF_CHUNK):
        u = jnp.maximum(_dot(hn, w1_ref[:, c0:c0 + FF_CHUNK]), 0.0)
        acc = acc + _dot((u * u).astype(BF16), w2_ref[c0:c0 + FF_CHUNK, :])
    if final_norm:
        acc = _rms(acc, fw_ref[...])
    out_ref[...] = acc


def _post(x2, oa, ob, proj, wa, wb, wo, nf, w1, w2, fw, final_norm):
    n = x2.shape[0]
    tm = POST_TM
    const = lambda i: (0, 0)
    single = pl.Buffered(1)
    kern = functools.partial(_post_kernel, final_norm=final_norm)
    return pl.pallas_call(
        kern,
        grid=(n // tm,),
        in_specs=[
            pl.BlockSpec((tm, D_MODEL), lambda i: (i, 0)),
            pl.BlockSpec((tm, D_MODEL), lambda i: (i, 0)),
            pl.BlockSpec((tm, D_MODEL), lambda i: (i, 0)),
            pl.BlockSpec((tm, PROJ_TN), lambda i: (i, SEG_G0)),
            pl.BlockSpec((tm, PROJ_TN), lambda i: (i, SEG_G1)),
            pl.BlockSpec((D_MODEL, D_MODEL), const, pipeline_mode=single),
            pl.BlockSpec((D_MODEL, D_MODEL), const, pipeline_mode=single),
            pl.BlockSpec((D_MODEL, D_MODEL), const, pipeline_mode=single),
            pl.BlockSpec((1, D_MODEL), const),
            pl.BlockSpec((D_MODEL, D_FF), const, pipeline_mode=single),
            pl.BlockSpec((D_FF, D_MODEL), const, pipeline_mode=single),
            pl.BlockSpec((1, D_MODEL), const),
        ],
        out_specs=pl.BlockSpec((tm, D_MODEL), lambda i: (i, 0)),
        out_shape=jax.ShapeDtypeStruct((n, D_MODEL), F32),
        compiler_params=pltpu.CompilerParams(
            dimension_semantics=("arbitrary",),
            vmem_limit_bytes=VMEM_LIMIT_BYTES),
        name="post_mixer",
    )(x2, oa, ob, proj, proj, wa, wb, wo, nf, w1, w2, fw)


def kernel(x, positions, norm_mix_w, w_in, ml_gate_b, conv_w, conv_b, da_lambda, da_subln_w,
           ml_norm_w, w_proj_a, w_proj_b, w_out, norm_ffn_w, w_ff1, w_ff2, final_norm_w):
    batch, seq, _ = x.shape
    n = batch * seq
    depth = w_in.shape[0]
    assert seq % ATT_T == 0 and seq % ML_CHUNK == 0 and seq % PROJ_TM == 0 and n % POST_TM == 0

    da_w = DA_HEADS * 2 * DA_HEAD_DIM
    ml_qk = ML_HEADS * ML_QK_DIM
    ml_v = ML_HEADS * ML_V_DIM
    o_mq = 3 * da_w
    o_mv = o_mq + 2 * ml_qk
    o_gi = o_mv + ml_v
    o_mo = o_gi + 2 * ML_HEADS
    o_gate = o_mo + ml_v

    pos2 = positions.reshape(n, 1)
    inv = ROPE_THETA ** (-jnp.arange(0, ROPE_DIM, 2, dtype=F32) / ROPE_DIM)
    invf = jnp.tile(inv, LANES // (ROPE_DIM // 2)).reshape(1, LANES)

    h = x.reshape(n, D_MODEL)
    for l in range(depth):
        lambda_init = 0.8 - 0.6 * math.exp(-0.3 * l)
        w = w_in[l]
        wp = jnp.concatenate([w[:, :o_gi], w[:, o_mo:]], axis=1).astype(BF16)
        wg = jnp.pad(w[:, o_gi:o_mo], ((0, 0), (0, LANES - 2 * ML_HEADS))).astype(BF16)
        gate_b = jnp.pad(ml_gate_b[l], (0, LANES - 2 * ML_HEADS)).reshape(1, LANES)

        proj, gates = _inproj(h, pos2, norm_mix_w[l].reshape(1, D_MODEL), invf, conv_w[l],
                              conv_b[l].reshape(1, -1), wp, wg, seq)
        oa = _attention(proj, da_lambda[l], da_subln_w[l].reshape(1, DA_V_DIM), batch, seq,
                        lambda_init)
        ob = _mlstm(proj, gates, gate_b, ml_norm_w[l].reshape(1, -1), batch, seq)
        h = _post(h, oa, ob, proj,
                  w_proj_a[l].astype(BF16), w_proj_b[l].astype(BF16), w_out[l].astype(BF16),
                  norm_ffn_w[l].reshape(1, D_MODEL), w_ff1[l].astype(BF16),
                  w_ff2[l].astype(BF16), final_norm_w.reshape(1, D_MODEL),
                  final_norm=(l == depth - 1))
    return h.reshape(batch, seq, D_MODEL)
```

```python
import functools
import math

import jax
import jax.numpy as jnp
from jax import lax
from jax.experimental import pallas as pl
from jax.experimental.pallas import tpu as pltpu

F32 = jnp.float32
BF16 = jnp.bfloat16

D_MODEL = 1024
DA_HEADS = 8
DA_HEAD_DIM = 64
DA_V_DIM = 128
ROPE_DIM = 16
ROPE_THETA = 500000.0
ML_HEADS = 8
ML_QK_DIM = 64
ML_V_DIM = 128
CONV_WIDTH = 4
D_FF = 4 * D_MODEL
EPS = 1e-6

LANES = 128
SUBLANES = 8
VMEM_LIMIT_BYTES = 56 * 1024 * 1024

PROJ_TM = 512
PROJ_TN = 1024
PROJ_CN = 256
ATT_T = 512
ATT_STEPS_PER_REGION = 9
ML_CHUNK = 256
POST_TM = 512
FF_CHUNK = 1024

SEG_Q, SEG_K, SEG_V, SEG_MQK, SEG_MV, SEG_MO, SEG_G0, SEG_G1 = range(8)
N_SEG = 8

NEG_BIG = -1e30
LOG2E = 1.4426950408889634


def _dot(a, b):
    return jnp.dot(a, b, preferred_element_type=F32)


def _dot_nt(a, b):
    return lax.dot_general(a, b, (((1,), (1,)), ((), ())), preferred_element_type=F32)


def _dot_tn(a, b):
    return lax.dot_general(a, b, (((0,), (0,)), ((), ())), preferred_element_type=F32)


def _sigmoid(x):
    return 0.5 * jnp.tanh(0.5 * x) + 0.5


def _rms(x, w):
    return x * lax.rsqrt(jnp.mean(x * x, axis=-1, keepdims=True) + EPS) * w


def _inproj_kernel(x_ref, pos_ref, nw_ref, invf_ref, cw_ref, cb_ref, w_ref, wg_ref,
                   out_ref, gates_ref, cbuf, *, tiles_per_seq):
    tm = PROJ_TM
    cn = PROJ_CN
    i = pl.program_id(0)

    xn = _rms(x_ref[...], nw_ref[...]).astype(BF16)
    gates_ref[...] = _dot(xn, wg_ref[...])

    ang = pos_ref[...].astype(F32) * invf_ref[...]
    c = jnp.cos(ang)
    s = jnp.sin(ang)
    lane = lax.broadcasted_iota(jnp.int32, (1, LANES), 1) % DA_HEAD_DIM
    half = ROPE_DIM // 2
    cos = jnp.where(lane < ROPE_DIM, c, 1.0)
    sina = jnp.where(lane < half, -s, 0.0)
    sinb = jnp.where((lane >= half) & (lane < ROPE_DIM), s, 0.0)

    hist_rows = CONV_WIDTH - 1
    first = i % tiles_per_seq == 0

    @pl.when(first)
    def _():
        cbuf[0:SUBLANES, :] = jnp.zeros((SUBLANES, PROJ_TN), F32)

    @pl.when(jnp.logical_not(first))
    def _():
        cbuf[0:SUBLANES, :] = cbuf[tm:tm + SUBLANES, :]

    seg_order = (SEG_V, SEG_Q, SEG_MV, SEG_MQK, SEG_G0, SEG_K, SEG_MO, SEG_G1)
    for c_in_seg in range(0, PROJ_TN, cn):
        for seg in seg_order:
            c0 = seg * PROJ_TN + c_in_seg
            acc = _dot(xn, w_ref[:, c0:c0 + cn])
            if seg in (SEG_Q, SEG_K):
                scale = DA_HEAD_DIM ** -0.5 * LOG2E if seg == SEG_Q else 1.0
                for l0 in range(0, cn, LANES):
                    xc = acc[:, l0:l0 + LANES]
                    r = (xc * cos + pltpu.roll(xc, LANES - half, 1) * sina
                         + pltpu.roll(xc, half, 1) * sinb)
                    out_ref[:, c0 + l0:c0 + l0 + LANES] = (r * scale).astype(BF16)
            elif seg == SEG_MQK:
                m0 = c0 - SEG_MQK * PROJ_TN
                cbuf[SUBLANES:SUBLANES + tm, m0:m0 + cn] = acc
                conv = cb_ref[:, m0:m0 + cn]
                for j in range(CONV_WIDTH):
                    off = SUBLANES - hist_rows + j
                    conv = conv + (cw_ref[j:j + 1, m0:m0 + cn]
                                   * cbuf[off:off + tm, m0:m0 + cn])
                y = conv * _sigmoid(conv)
                if m0 < ML_HEADS * ML_QK_DIM:
                    y = y * (ML_QK_DIM ** -0.5)
                out_ref[:, c0:c0 + cn] = y.astype(BF16)
            elif seg in (SEG_V, SEG_MV):
                out_ref[:, c0:c0 + cn] = acc.astype(BF16)
            else:
                out_ref[:, c0:c0 + cn] = _sigmoid(acc).astype(BF16)


def _inproj(x2, pos2, nw, invf, conv_w, conv_b, wp, wg, seq):
    n = x2.shape[0]
    tm = PROJ_TM
    const = lambda i: (0, 0)
    single = pl.Buffered(1)
    kern = functools.partial(_inproj_kernel, tiles_per_seq=seq // tm)
    return pl.pallas_call(
        kern,
        grid=(n // tm,),
        in_specs=[
            pl.BlockSpec((tm, D_MODEL), lambda i: (i, 0)),
            pl.BlockSpec((tm, 1), lambda i: (i, 0)),
            pl.BlockSpec((1, D_MODEL), const),
            pl.BlockSpec((1, LANES), const),
            pl.BlockSpec((CONV_WIDTH, PROJ_TN), const),
            pl.BlockSpec((1, PROJ_TN), const),
            pl.BlockSpec((D_MODEL, N_SEG * PROJ_TN), const, pipeline_mode=single),
            pl.BlockSpec((D_MODEL, LANES), const, pipeline_mode=single),
        ],
        out_specs=[
            pl.BlockSpec((tm, N_SEG * PROJ_TN), lambda i: (i, 0)),
            pl.BlockSpec((tm, LANES), lambda i: (i, 0)),
        ],
        out_shape=[
            jax.ShapeDtypeStruct((n, N_SEG * PROJ_TN), BF16),
            jax.ShapeDtypeStruct((n, LANES), F32),
        ],
        scratch_shapes=[
            pltpu.VMEM((tm + SUBLANES, PROJ_TN), F32),
        ],
        compiler_params=pltpu.CompilerParams(
            dimension_semantics=("arbitrary",),
            vmem_limit_bytes=VMEM_LIMIT_BYTES),
        name="inproj",
    )(x2, pos2, nw, invf, conv_w, conv_b, wp, wg)


def _attn_kernel(lam_ref, sw_ref, q_ref, k_ref, v_ref, o_ref, vext_sc, sa_sc, sb_sc, pa_sc,
                 pb_sc, acc_sc, m_sc, *, seq, lambda_init):
    t = ATT_T
    hd = DA_HEAD_DIM
    dv = DA_V_DIM
    nq = seq // t
    rows = 2 * t
    nlt = t // LANES

    @pl.when((pl.program_id(0) == 0) & (pl.program_id(1) == 0))
    def _():
        vext_sc[:, dv:] = jnp.ones((seq, LANES), BF16)

    vext_sc[:, :dv] = v_ref[...]

    lp = lam_ref[...]
    lam = (jnp.exp(jnp.sum(lp[0:1] * lp[1:2], axis=-1, keepdims=True))
           - jnp.exp(jnp.sum(lp[2:3] * lp[3:4], axis=-1, keepdims=True)) + lambda_init)

    def blk(i):
        return pl.ds(i * t if isinstance(i, int) else pl.multiple_of(i * t, t), t)

    def scores(dst_s, dst_pm, qi, kblk, masked):
        q = q_ref[blk(qi), :]
        lane = lax.broadcasted_iota(jnp.int32, (t, LANES), 1)
        zero = jnp.zeros_like(q)
        qs = jnp.concatenate([jnp.where(lane < hd, q, zero), jnp.where(lane >= hd, q, zero)],
                             axis=0)
        kb = k_ref[blk(kblk), :]
        s = _dot_nt(qs, kb)
        if masked:
            keep = (lax.broadcasted_iota(jnp.int32, (t, t), 1)
                    <= lax.broadcasted_iota(jnp.int32, (t, t), 0))
            s = jnp.where(jnp.concatenate([keep, keep], axis=0), s, NEG_BIG)
        dst_s[...] = s
        pm = s[:, 0:LANES]
        for c in range(1, nlt):
            pm = jnp.maximum(pm, s[:, c * LANES:(c + 1) * LANES])
        dst_pm[...] = pm

    def process(src_s, src_pm, qi, vblk, first):
        mx = jnp.max(src_pm[...], axis=-1, keepdims=True)
        if first:
            m_new = jnp.broadcast_to(mx, (rows, LANES))
        else:
            m_prev = m_sc[qi]
            m_new = jnp.maximum(m_prev, mx)
            alpha = jnp.exp2(m_prev - m_new)
        p = jnp.concatenate(
            [jnp.exp2(src_s[:, c * LANES:(c + 1) * LANES] - m_new).astype(BF16)
             for c in range(nlt)], axis=1)
        pv = _dot(p, vext_sc[blk(vblk), :])
        if first:
            acc_sc[qi] = pv
        else:
            acc_sc[qi] = jnp.concatenate([alpha, alpha], axis=1) * acc_sc[qi] + pv
        m_sc[qi] = m_new

    def finalize(qi):
        acc = acc_sc[qi]
        o = acc[:, :dv] / acc[:, dv:]
        od = o[:t] - lam * o[t:]
        y = _rms(od, sw_ref[...]) * (1.0 - lambda_init)
        o_ref[qi * t:(qi + 1) * t, :] = y.astype(BF16)

    bufs = ((sa_sc, pa_sc), (sb_sc, pb_sc))
    steps = []
    for qi in range(nq):
        steps.append((qi, qi, True))
        steps.extend((qi, j, False) for j in range(qi))
    qi0, kb0, diag0 = steps[0]
    scores(*bufs[0], qi0, kb0, diag0)

    def emit(g0, g1):
        for g in range(g0, g1):
            qi, kblk, diag = steps[g]
            if g + 1 < len(steps):
                qn, kn, dn = steps[g + 1]
                scores(*bufs[(g + 1) % 2], qn, kn, dn)
            process(*bufs[g % 2], qi, kblk, diag)
            if g + 1 == len(steps) or steps[g + 1][0] != qi:
                finalize(qi)

    one = jnp.minimum(pl.program_id(0), 0) + 1
    for g0 in range(0, len(steps), ATT_STEPS_PER_REGION):
        g1 = min(g0 + ATT_STEPS_PER_REGION, len(steps))

        def region(_, c, g0=g0, g1=g1):
            emit(g0, g1)
            return c

        lax.fori_loop(0, one, region, 0)


def _attention(proj, lam_p, subln_w, batch, seq, lambda_init):
    n = proj.shape[0]
    kern = functools.partial(_attn_kernel, seq=seq, lambda_init=lambda_init)
    hb = PROJ_TN // LANES
    return pl.pallas_call(
        kern,
        grid=(batch, DA_HEADS),
        in_specs=[
            pl.BlockSpec((4, DA_HEAD_DIM), lambda b, h: (0, 0)),
            pl.BlockSpec((1, DA_V_DIM), lambda b, h: (0, 0)),
            pl.BlockSpec((seq, LANES), lambda b, h: (b, SEG_Q * hb + h)),
            pl.BlockSpec((seq, LANES), lambda b, h: (b, SEG_K * hb + h)),
            pl.BlockSpec((seq, LANES), lambda b, h: (b, SEG_V * hb + h)),
        ],
        out_specs=pl.BlockSpec((seq, LANES), lambda b, h: (b, h)),
        out_shape=jax.ShapeDtypeStruct((n, DA_HEADS * DA_V_DIM), BF16),
        scratch_shapes=[
            pltpu.VMEM((seq, 2 * LANES), BF16),
            pltpu.VMEM((2 * ATT_T, ATT_T), F32),
            pltpu.VMEM((2 * ATT_T, ATT_T), F32),
            pltpu.VMEM((2 * ATT_T, LANES), F32),
            pltpu.VMEM((2 * ATT_T, LANES), F32),
            pltpu.VMEM((seq // ATT_T, 2 * ATT_T, 2 * LANES), F32),
            pltpu.VMEM((seq // ATT_T, 2 * ATT_T, LANES), F32),
        ],
        compiler_params=pltpu.CompilerParams(
            dimension_semantics=("arbitrary", "arbitrary"),
            vmem_limit_bytes=VMEM_LIMIT_BYTES),
        name="diff_attention",
    )(lam_p, subln_w, proj, proj, proj)


def _mlstm_kernel(qk_ref, v_ref, so_ref, g_ref, gb_ref, nw_ref, out_ref, cext_sc, m_sc,
                  mask_sc):
    L = ML_CHUNK
    dk, dv = ML_QK_DIM, ML_V_DIM
    qkw = ML_HEADS * dk
    nlt = L // LANES
    c = pl.program_id(1)

    @pl.when(c == 0)
    def _():
        cext_sc[...] = jnp.zeros(cext_sc.shape, F32)
        m_sc[...] = jnp.zeros(m_sc.shape, F32)
        row = lax.broadcasted_iota(jnp.int32, (L, L), 0)
        col = lax.broadcasted_iota(jnp.int32, (L, L), 1)
        mask_sc[...] = jnp.where(col <= row, 0.0, NEG_BIG)

    g = g_ref[...] + gb_ref[...]
    logf = jnp.minimum(g, 0.0) - jnp.log(1.0 + jnp.exp(-jnp.abs(g)))
    tri_b = (mask_sc[...] == 0.0).astype(BF16)
    hi = logf.astype(BF16)
    r1 = logf - hi.astype(F32)
    mid = r1.astype(BF16)
    lo = (r1 - mid.astype(F32)).astype(BF16)
    bcs = _dot(tri_b, hi) + _dot(tri_b, mid) + _dot(tri_b, lo)
    b_al = pltpu.roll(bcs, LANES - ML_HEADS, 1)
    a = g - b_al
    m_prev = m_sc[0:1, :]
    rowi = lax.broadcasted_iota(jnp.int32, (L, LANES), 0)
    cm = a
    d = 1
    while d < L:
        cm = jnp.maximum(cm, jnp.where(rowi >= d, pltpu.roll(cm, d, 0), NEG_BIG))
        d *= 2
    u = jnp.maximum(cm, m_prev)
    mt = b_al + u
    b_last = b_al[L - 1:L, :]
    w_log = b_last + a
    m_new = jnp.maximum(b_last + m_prev, jnp.max(w_log, axis=0, keepdims=True))
    decay = jnp.exp(b_last + m_prev - m_new)
    ws = jnp.exp(w_log - m_new)
    m_sc[0:1, :] = m_new
    a_t = a.T
    ws_t = ws.T
    ones_blk = jnp.ones((L, LANES), BF16)

    for h in range(ML_HEADS):
        u_b = jnp.broadcast_to(u[:, h:h + 1], (L, LANES))
        mt_b = jnp.broadcast_to(mt[:, h:h + 1], (L, LANES))
        inter_b = jnp.exp(m_prev[:, h:h + 1] - u_b)
        floor_b = jnp.exp(-mt_b)
        dexp = jnp.concatenate(
            [jnp.exp(a_t[h:h + 1, t * LANES:(t + 1) * LANES] - u_b
                     + mask_sc[:, t * LANES:(t + 1) * LANES]) for t in range(nlt)], axis=1)
        qh = qk_ref[:, h * dk:(h + 1) * dk]
        kh_t = qk_ref[:, qkw + h * dk:qkw + (h + 1) * dk].T
        vext = jnp.concatenate([v_ref[:, h * dv:(h + 1) * dv], ones_blk], axis=1)
        s = (_dot(qh, kh_t) * dexp).astype(BF16)
        cext = cext_sc[h]
        hext = (_dot(s, vext)
                + jnp.concatenate([inter_b, inter_b], axis=1) * _dot(qh, cext.astype(BF16)))
        hm = hext[:, :dv] / jnp.maximum(jnp.abs(hext[:, dv:]), floor_b)
        y = _rms(hm, nw_ref[:, h * dv:(h + 1) * dv])
        y = so_ref[:, h * dv:(h + 1) * dv].astype(F32) * y
        out_ref[:, h * dv:(h + 1) * dv] = y.astype(BF16)
        kw_t = (kh_t.astype(F32) * ws_t[h:h + 1, :]).astype(BF16)
        cext_sc[h] = decay[:, h:h + 1] * cext + _dot(kw_t, vext)


def _mlstm(proj, gates, gate_b, norm_w, batch, seq):
    n = proj.shape[0]
    L = ML_CHUNK
    nc = seq // L
    w = ML_HEADS * ML_V_DIM
    return pl.pallas_call(
        _mlstm_kernel,
        grid=(batch, nc),
        in_specs=[
            pl.BlockSpec((L, PROJ_TN), lambda b, c: (b * nc + c, SEG_MQK)),
            pl.BlockSpec((L, PROJ_TN), lambda b, c: (b * nc + c, SEG_MV)),
            pl.BlockSpec((L, PROJ_TN), lambda b, c: (b * nc + c, SEG_MO)),
            pl.BlockSpec((L, LANES), lambda b, c: (b * nc + c, 0)),
            pl.BlockSpec((1, LANES), lambda b, c: (0, 0)),
            pl.BlockSpec((1, w), lambda b, c: (0, 0)),
        ],
        out_specs=pl.BlockSpec((L, w), lambda b, c: (b * nc + c, 0)),
        out_shape=jax.ShapeDtypeStruct((n, w), BF16),
        scratch_shapes=[
            pltpu.VMEM((ML_HEADS, ML_QK_DIM, 2 * LANES), F32),
            pltpu.VMEM((SUBLANES, LANES), F32),
            pltpu.VMEM((L, L), F32),
        ],
        compiler_params=pltpu.CompilerParams(
            dimension_semantics=("arbitrary", "arbitrary"),
            vmem_limit_bytes=VMEM_LIMIT_BYTES),
        name="mlstm",
    )(proj, proj, proj, gates, gate_b, norm_w)


def _post_kernel(x_ref, oa_ref, ob_ref, g0_ref, g1_ref, wa_ref, wb_ref, wo_ref, nf_ref,
                 w1_ref, w2_ref, fw_ref, out_ref, *, final_norm):
    ya = _dot(oa_ref[...], wa_ref[...])
    yb = _dot(ob_ref[...], wb_ref[...])
    merged = g0_ref[...].astype(F32) * ya + g1_ref[...].astype(F32) * yb
    h = x_ref[...] + _dot(merged.astype(BF16), wo_ref[...])
    hn = _rms(h, nf_ref[...]).astype(BF16)
    acc = h
    for c0 in range(0, D_FF, FF_CHUNK):
        u = jnp.maximum(_dot(hn, w1_ref[:, c0:c0 + FF_CHUNK]), 0.0)
        acc = acc + _dot((u * u).astype(BF16), w2_ref[c0:c0 + FF_CHUNK, :])
    if final_norm:
        acc = _rms(acc, fw_ref[...])
    out_ref[...] = acc


def _post(x2, oa, ob, proj, wa, wb, wo, nf, w1, w2, fw, final_norm):
    n = x2.shape[0]
    tm = POST_TM
    const = lambda i: (0, 0)
    single = pl.Buffered(1)
    kern = functools.partial(_post_kernel, final_norm=final_norm)
    return pl.pallas_call(
        kern,
        grid=(n // tm,),
        in_specs=[
            pl.BlockSpec((tm, D_MODEL), lambda i: (i, 0)),
            pl.BlockSpec((tm, D_MODEL), lambda i: (i, 0)),
            pl.BlockSpec((tm, D_MODEL), lambda i: (i, 0)),
            pl.BlockSpec((tm, PROJ_TN), lambda i: (i, SEG_G0)),
            pl.BlockSpec((tm, PROJ_TN), lambda i: (i, SEG_G1)),
            pl.BlockSpec((D_MODEL, D_MODEL), const, pipeline_mode=single),
            pl.BlockSpec((D_MODEL, D_MODEL), const, pipeline_mode=single),
            pl.BlockSpec((D_MODEL, D_MODEL), const, pipeline_mode=single),
            pl.BlockSpec((1, D_MODEL), const),
            pl.BlockSpec((D_MODEL, D_FF), const, pipeline_mode=single),
            pl.BlockSpec((D_FF, D_MODEL), const, pipeline_mode=single),
            pl.BlockSpec((1, D_MODEL), const),
        ],
        out_specs=pl.BlockSpec((tm, D_MODEL), lambda i: (i, 0)),
        out_shape=jax.ShapeDtypeStruct((n, D_MODEL), F32),
        compiler_params=pltpu.CompilerParams(
            dimension_semantics=("arbitrary",),
            vmem_limit_bytes=VMEM_LIMIT_BYTES),
        name="post_mixer",
    )(x2, oa, ob, proj, proj, wa, wb, wo, nf, w1, w2, fw)


def kernel(x, positions, norm_mix_w, w_in, ml_gate_b, conv_w, conv_b, da_lambda, da_subln_w,
           ml_norm_w, w_proj_a, w_proj_b, w_out, norm_ffn_w, w_ff1, w_ff2, final_norm_w):
    batch, seq, _ = x.shape
    n = batch * seq
    depth = w_in.shape[0]
    assert seq % ATT_T == 0 and seq % ML_CHUNK == 0 and seq % PROJ_TM == 0 and n % POST_TM == 0

    da_w = DA_HEADS * 2 * DA_HEAD_DIM
    ml_qk = ML_HEADS * ML_QK_DIM
    ml_v = ML_HEADS * ML_V_DIM
    o_mq = 3 * da_w
    o_mv = o_mq + 2 * ml_qk
    o_gi = o_mv + ml_v
    o_mo = o_gi + 2 * ML_HEADS
    o_gate = o_mo + ml_v

    pos2 = positions.reshape(n, 1)
    inv = ROPE_THETA ** (-jnp.arange(0, ROPE_DIM, 2, dtype=F32) / ROPE_DIM)
    invf = jnp.tile(inv, LANES // (ROPE_DIM // 2)).reshape(1, LANES)

    h = x.reshape(n, D_MODEL)
    for l in range(depth):
        lambda_init = 0.8 - 0.6 * math.exp(-0.3 * l)
        w = w_in[l].astype(BF16)
        wp = jnp.concatenate([w[:, :o_gi], w[:, o_mo:]], axis=1)
        wg = jnp.pad(w[:, o_gi:o_mo], ((0, 0), (0, LANES - 2 * ML_HEADS)))
        gate_b = jnp.pad(ml_gate_b[l], (0, LANES - 2 * ML_HEADS)).reshape(1, LANES)

        proj, gates = _inproj(h, pos2, norm_mix_w[l].reshape(1, D_MODEL), invf, conv_w[l],
                              conv_b[l].reshape(1, -1), wp, wg, seq)
        oa = _attention(proj, da_lambda[l], da_subln_w[l].reshape(1, DA_V_DIM), batch, seq,
                        lambda_init)
        ob = _mlstm(proj, gates, gate_b, ml_norm_w[l].reshape(1, -1), batch, seq)
        h = _post(h, oa, ob, proj,
                  w_proj_a[l].astype(BF16), w_proj_b[l].astype(BF16), w_out[l].astype(BF16),
                  norm_ffn_w[l].reshape(1, D_MODEL), w_ff1[l].astype(BF16),
                  w_ff2[l].astype(BF16), final_norm_w.reshape(1, D_MODEL),
                  final_norm=(l == depth - 1))
    return h.reshape(batch, seq, D_MODEL)
```

```python
import functools
import math

import jax
import jax.numpy as jnp
from jax import lax
from jax.experimental import pallas as pl
from jax.experimental.pallas import tpu as pltpu

F32 = jnp.float32
BF16 = jnp.bfloat16

D_MODEL = 1024
DA_HEADS = 8
DA_HEAD_DIM = 64
DA_V_DIM = 128
ROPE_DIM = 16
ROPE_THETA = 500000.0
ML_HEADS = 8
ML_QK_DIM = 64
ML_V_DIM = 128
CONV_WIDTH = 4
D_FF = 4 * D_MODEL
EPS = 1e-6

LANES = 128
SUBLANES = 8
VMEM_LIMIT_BYTES = 56 * 1024 * 1024

PROJ_TM = 512
PROJ_TN = 1024
PROJ_CN = 256
PROJ_TABLE_PIECES = 8
ATT_T = 512
ATT_STEPS_PER_REGION = 9
ML_CHUNK = 256
ML_CHUNKS_PER_STEP = 1
POST_TM = 512
FF_CHUNK = 1024

SEG_Q, SEG_K, SEG_V, SEG_MQK, SEG_MV, SEG_MO, SEG_G0, SEG_G1 = range(8)
N_SEG = 8

NEG_BIG = -1e30
LOG2E = 1.4426950408889634


def _dot(a, b):
    return jnp.dot(a, b, preferred_element_type=F32)


def _dot_nt(a, b):
    return lax.dot_general(a, b, (((1,), (1,)), ((), ())), preferred_element_type=F32)


def _dot_tn(a, b):
    return lax.dot_general(a, b, (((0,), (0,)), ((), ())), preferred_element_type=F32)


def _sigmoid(x):
    return 0.5 * jnp.tanh(0.5 * x) + 0.5


def _rms(x, w):
    return x * lax.rsqrt(jnp.mean(x * x, axis=-1, keepdims=True) + EPS) * w


def _inproj_kernel(x_ref, pos_ref, nw_ref, invf_ref, cw_ref, cb_ref, w_ref, wg_ref,
                   out_ref, gates_ref, cbuf, *, tiles_per_seq):
    tm = PROJ_TM
    cn = PROJ_CN
    i = pl.program_id(0)

    hist_rows = CONV_WIDTH - 1
    first = i % tiles_per_seq == 0

    @pl.when(first)
    def _():
        cbuf[0:SUBLANES, :] = jnp.zeros((SUBLANES, PROJ_TN), F32)

    @pl.when(jnp.logical_not(first))
    def _():
        cbuf[0:SUBLANES, :] = cbuf[tm:tm + SUBLANES, :]

    xn = _rms(x_ref[...], nw_ref[...]).astype(BF16)
    gates_ref[...] = _dot(xn, wg_ref[...])
    half = ROPE_DIM // 2

    def chunk(seg, c_in_seg, tables):
        c0 = seg * PROJ_TN + c_in_seg
        acc = _dot(xn, w_ref[:, c0:c0 + cn])
        if seg in (SEG_Q, SEG_K):
            cos, sina, sinb = tables
            scale = DA_HEAD_DIM ** -0.5 * LOG2E if seg == SEG_Q else 1.0
            for l0 in range(0, cn, LANES):
                xc = acc[:, l0:l0 + LANES]
                r = (xc * cos + pltpu.roll(xc, LANES - half, 1) * sina
                     + pltpu.roll(xc, half, 1) * sinb)
                out_ref[:, c0 + l0:c0 + l0 + LANES] = (r * scale).astype(BF16)
        elif seg == SEG_MQK:
            m0 = c0 - SEG_MQK * PROJ_TN
            cbuf[SUBLANES:SUBLANES + tm, m0:m0 + cn] = acc
            conv = cb_ref[:, m0:m0 + cn]
            for j in range(CONV_WIDTH):
                off = SUBLANES - hist_rows + j
                conv = conv + (cw_ref[j:j + 1, m0:m0 + cn]
                               * cbuf[off:off + tm, m0:m0 + cn])
            y = conv * _sigmoid(conv)
            if m0 < ML_HEADS * ML_QK_DIM:
                y = y * (ML_QK_DIM ** -0.5)
            out_ref[:, c0:c0 + cn] = y.astype(BF16)
        elif seg in (SEG_V, SEG_MV):
            out_ref[:, c0:c0 + cn] = acc.astype(BF16)
        else:
            out_ref[:, c0:c0 + cn] = _sigmoid(acc).astype(BF16)

    lane = lax.broadcasted_iota(jnp.int32, (1, LANES), 1) % DA_HEAD_DIM

    def table_piece(r0, r1):
        ang = pos_ref[r0:r1, :].astype(F32) * invf_ref[...]
        c = jnp.cos(ang)
        s = jnp.sin(ang)
        return (jnp.where(lane < ROPE_DIM, c, 1.0), jnp.where(lane < half, -s, 0.0),
                jnp.where((lane >= half) & (lane < ROPE_DIM), s, 0.0))

    piece_rows = tm // PROJ_TABLE_PIECES
    pieces = []
    for c_in_seg in range(0, PROJ_TN, cn):
        for seg in (SEG_V, SEG_MQK, SEG_MV, SEG_G0, SEG_MO, SEG_G1):
            chunk(seg, c_in_seg, None)
            if len(pieces) < PROJ_TABLE_PIECES:
                r0 = len(pieces) * piece_rows
                pieces.append(table_piece(r0, r0 + piece_rows))
    tables = tuple(jnp.concatenate([p[k] for p in pieces], axis=0) for k in range(3))
    for c_in_seg in range(0, PROJ_TN, cn):
        for seg in (SEG_Q, SEG_K):
            chunk(seg, c_in_seg, tables)


def _inproj(x2, pos2, nw, invf, conv_w, conv_b, wp, wg, seq):
    n = x2.shape[0]
    tm = PROJ_TM
    const = lambda i: (0, 0)
    single = pl.Buffered(1)
    kern = functools.partial(_inproj_kernel, tiles_per_seq=seq // tm)
    return pl.pallas_call(
        kern,
        grid=(n // tm,),
        in_specs=[
            pl.BlockSpec((tm, D_MODEL), lambda i: (i, 0)),
            pl.BlockSpec((tm, 1), lambda i: (i, 0)),
            pl.BlockSpec((1, D_MODEL), const),
            pl.BlockSpec((1, LANES), const),
            pl.BlockSpec((CONV_WIDTH, PROJ_TN), const),
            pl.BlockSpec((1, PROJ_TN), const),
            pl.BlockSpec((D_MODEL, N_SEG * PROJ_TN), const, pipeline_mode=single),
            pl.BlockSpec((D_MODEL, LANES), const, pipeline_mode=single),
        ],
        out_specs=[
            pl.BlockSpec((tm, N_SEG * PROJ_TN), lambda i: (i, 0)),
            pl.BlockSpec((tm, LANES), lambda i: (i, 0)),
        ],
        out_shape=[
            jax.ShapeDtypeStruct((n, N_SEG * PROJ_TN), BF16),
            jax.ShapeDtypeStruct((n, LANES), F32),
        ],
        scratch_shapes=[
            pltpu.VMEM((tm + SUBLANES, PROJ_TN), F32),
        ],
        compiler_params=pltpu.CompilerParams(
            dimension_semantics=("arbitrary",),
            vmem_limit_bytes=VMEM_LIMIT_BYTES),
        name="inproj",
    )(x2, pos2, nw, invf, conv_w, conv_b, wp, wg)


def _attn_kernel(lam_ref, sw_ref, q_ref, k_ref, v_ref, o_ref, vext_sc, sa_sc, sb_sc, pa_sc,
                 pb_sc, acc_sc, m_sc, *, seq, lambda_init):
    t = ATT_T
    hd = DA_HEAD_DIM
    dv = DA_V_DIM
    nq = seq // t
    rows = 2 * t

    @pl.when((pl.program_id(0) == 0) & (pl.program_id(1) == 0))
    def _():
        vext_sc[:, dv:] = jnp.ones((seq, LANES), BF16)

    vext_sc[:, :dv] = v_ref[...]

    lp = lam_ref[...]
    lam = (jnp.exp(jnp.sum(lp[0:1] * lp[1:2], axis=-1, keepdims=True))
           - jnp.exp(jnp.sum(lp[2:3] * lp[3:4], axis=-1, keepdims=True)) + lambda_init)

    def blk(i):
        return pl.ds(i * t if isinstance(i, int) else pl.multiple_of(i * t, t), t)

    def split_q(qi):
        q = q_ref[blk(qi), :]
        lane = lax.broadcasted_iota(jnp.int32, (t, LANES), 1)
        zero = jnp.zeros_like(q)
        return jnp.where(lane < hd, q, zero), jnp.where(lane >= hd, q, zero)

    def lane_tile_max(s):
        pm = s[:, 0:LANES]
        for c in range(1, s.shape[1] // LANES):
            pm = jnp.maximum(pm, s[:, c * LANES:(c + 1) * LANES])
        return pm

    def exp_tiles(src, r0, r1, width, m):
        return jnp.concatenate(
            [jnp.exp2(src[r0:r1, c * LANES:(c + 1) * LANES] - m).astype(BF16)
             for c in range(width // LANES)], axis=1)

    def scores(dst_s, dst_pm, qi, kblk):
        q0, q1 = split_q(qi)
        s = _dot_nt(jnp.concatenate([q0, q1], axis=0), k_ref[blk(kblk), :])
        dst_s[...] = s
        dst_pm[...] = lane_tile_max(s)

    def process(src_s, src_pm, qi, vblk):
        m_prev = m_sc[qi]
        m_new = jnp.maximum(m_prev, jnp.max(src_pm[...], axis=-1, keepdims=True))
        alpha = jnp.exp2(m_prev - m_new)
        p = exp_tiles(src_s, 0, rows, t, m_new)
        pv = _dot(p, vext_sc[blk(vblk), :])
        acc_sc[qi] = jnp.concatenate([alpha, alpha], axis=1) * acc_sc[qi] + pv
        m_sc[qi] = m_new

    hh = t // 2

    def scores_diag(dst_s, dst_pm, qi):
        q0, q1 = split_q(qi)
        k_all = k_ref[qi * t:(qi + 1) * t, :]
        s_a = _dot_nt(jnp.concatenate([q0[:hh], q1[:hh]], axis=0), k_all[:hh])
        s_b = _dot_nt(jnp.concatenate([q0[hh:], q1[hh:]], axis=0), k_all)
        keep_a = (lax.broadcasted_iota(jnp.int32, (hh, hh), 1)
                  <= lax.broadcasted_iota(jnp.int32, (hh, hh), 0))
        keep_b = (lax.broadcasted_iota(jnp.int32, (hh, t), 1)
                  <= lax.broadcasted_iota(jnp.int32, (hh, t), 0) + hh)
        s_a = jnp.where(jnp.concatenate([keep_a, keep_a], axis=0), s_a, NEG_BIG)
        s_b = jnp.where(jnp.concatenate([keep_b, keep_b], axis=0), s_b, NEG_BIG)
        dst_s[0:t, 0:hh] = s_a
        dst_s[t:rows, :] = s_b
        dst_pm[0:t, :] = lane_tile_max(s_a)
        dst_pm[t:rows, :] = lane_tile_max(s_b)

    def process_diag(src_s, src_pm, qi):
        m_a = jnp.broadcast_to(jnp.max(src_pm[0:t, :], axis=-1, keepdims=True), (t, LANES))
        m_b = jnp.broadcast_to(jnp.max(src_pm[t:rows, :], axis=-1, keepdims=True), (t, LANES))
        pv_a = _dot(exp_tiles(src_s, 0, t, hh, m_a), vext_sc[qi * t:qi * t + hh, :])
        pv_b = _dot(exp_tiles(src_s, t, rows, t, m_b), vext_sc[qi * t:(qi + 1) * t, :])
        for dst, val_acc, val_m in ((0, pv_a[:hh], m_a[:hh]), (hh, pv_b[:hh], m_b[:hh]),
                                    (t, pv_a[hh:], m_a[hh:]), (t + hh, pv_b[hh:], m_b[hh:])):
            acc_sc[qi, dst:dst + hh, :] = val_acc
            m_sc[qi, dst:dst + hh, :] = val_m

    def finalize(qi):
        acc = acc_sc[qi]
        o = acc[:, :dv] / acc[:, dv:]
        od = o[:t] - lam * o[t:]
        y = _rms(od, sw_ref[...]) * (1.0 - lambda_init)
        o_ref[qi * t:(qi + 1) * t, :] = y.astype(BF16)

    bufs = ((sa_sc, pa_sc), (sb_sc, pb_sc))
    steps = []
    for qi in range(nq):
        steps.append((qi, qi, True))
        steps.extend((qi, j, False) for j in range(qi))
    def emit_scores(g):
        qi, kblk, diag = steps[g]
        if diag:
            scores_diag(*bufs[g % 2], qi)
        else:
            scores(*bufs[g % 2], qi, kblk)

    emit_scores(0)

    def emit(g0, g1):
        for g in range(g0, g1):
            qi, kblk, diag = steps[g]
            if g + 1 < len(steps):
                emit_scores(g + 1)
            if diag:
                process_diag(*bufs[g % 2], qi)
            else:
                process(*bufs[g % 2], qi, kblk)
            if g + 1 == len(steps) or steps[g + 1][0] != qi:
                finalize(qi)

    one = jnp.minimum(pl.program_id(0), 0) + 1
    for g0 in range(0, len(steps), ATT_STEPS_PER_REGION):
        g1 = min(g0 + ATT_STEPS_PER_REGION, len(steps))

        def region(_, c, g0=g0, g1=g1):
            emit(g0, g1)
            return c

        lax.fori_loop(0, one, region, 0)


def _attention(proj, lam_p, subln_w, batch, seq, lambda_init):
    n = proj.shape[0]
    kern = functools.partial(_attn_kernel, seq=seq, lambda_init=lambda_init)
    hb = PROJ_TN // LANES
    return pl.pallas_call(
        kern,
        grid=(batch, DA_HEADS),
        in_specs=[
            pl.BlockSpec((4, DA_HEAD_DIM), lambda b, h: (0, 0)),
            pl.BlockSpec((1, DA_V_DIM), lambda b, h: (0, 0)),
            pl.BlockSpec((seq, LANES), lambda b, h: (b, SEG_Q * hb + h)),
            pl.BlockSpec((seq, LANES), lambda b, h: (b, SEG_K * hb + h)),
            pl.BlockSpec((seq, LANES), lambda b, h: (b, SEG_V * hb + h)),
        ],
        out_specs=pl.BlockSpec((seq, LANES), lambda b, h: (b, h)),
        out_shape=jax.ShapeDtypeStruct((n, DA_HEADS * DA_V_DIM), BF16),
        scratch_shapes=[
            pltpu.VMEM((seq, 2 * LANES), BF16),
            pltpu.VMEM((2 * ATT_T, ATT_T), F32),
            pltpu.VMEM((2 * ATT_T, ATT_T), F32),
            pltpu.VMEM((2 * ATT_T, LANES), F32),
            pltpu.VMEM((2 * ATT_T, LANES), F32),
            pltpu.VMEM((seq // ATT_T, 2 * ATT_T, 2 * LANES), F32),
            pltpu.VMEM((seq // ATT_T, 2 * ATT_T, LANES), F32),
        ],
        compiler_params=pltpu.CompilerParams(
            dimension_semantics=("arbitrary", "arbitrary"),
            vmem_limit_bytes=VMEM_LIMIT_BYTES),
        name="diff_attention",
    )(lam_p, subln_w, proj, proj, proj)


def _mlstm_kernel(qk_ref, v_ref, so_ref, g_ref, gb_ref, nw_ref, out_ref, cext_sc, m_sc,
                  mask_sc):
    L = ML_CHUNK
    dk, dv = ML_QK_DIM, ML_V_DIM
    qkw = ML_HEADS * dk
    nlt = L // LANES
    c = pl.program_id(1)

    @pl.when(c == 0)
    def _():
        cext_sc[...] = jnp.zeros(cext_sc.shape, F32)
        m_sc[...] = jnp.zeros(m_sc.shape, F32)
        row = lax.broadcasted_iota(jnp.int32, (L, L), 0)
        col = lax.broadcasted_iota(jnp.int32, (L, L), 1)
        mask_sc[...] = jnp.where(col <= row, 0.0, NEG_BIG)

    tri_b = (mask_sc[...] == 0.0).astype(BF16)
    rowi = lax.broadcasted_iota(jnp.int32, (L, LANES), 0)
    ones_blk = jnp.ones((L, LANES), BF16)

    def chunk(r0):
        rs = slice(r0, r0 + L)
        g = g_ref[rs, :] + gb_ref[...]
        logf = jnp.minimum(g, 0.0) - jnp.log(1.0 + jnp.exp(-jnp.abs(g)))
        hi = logf.astype(BF16)
        r1 = logf - hi.astype(F32)
        mid = r1.astype(BF16)
        lo = (r1 - mid.astype(F32)).astype(BF16)
        bcs = _dot(tri_b, hi) + _dot(tri_b, mid) + _dot(tri_b, lo)
        b_al = pltpu.roll(bcs, LANES - ML_HEADS, 1)
        a = g - b_al
        m_prev = m_sc[0:1, :]
        cm = a
        d = 1
        while d < L:
            cm = jnp.maximum(cm, jnp.where(rowi >= d, pltpu.roll(cm, d, 0), NEG_BIG))
            d *= 2
        u = jnp.maximum(cm, m_prev)
        mt = b_al + u
        b_last = b_al[L - 1:L, :]
        w_log = b_last + a
        m_new = jnp.maximum(b_last + m_prev, jnp.max(w_log, axis=0, keepdims=True))
        decay = jnp.exp(b_last + m_prev - m_new)
        ws = jnp.exp(w_log - m_new)
        m_sc[0:1, :] = m_new
        a_t = a.T
        ws_t = ws.T

        for h in range(ML_HEADS):
            u_b = jnp.broadcast_to(u[:, h:h + 1], (L, LANES))
            mt_b = jnp.broadcast_to(mt[:, h:h + 1], (L, LANES))
            inter_b = jnp.exp(m_prev[:, h:h + 1] - u_b)
            floor_b = jnp.exp(-mt_b)
            dexp = jnp.concatenate(
                [jnp.exp(a_t[h:h + 1, t * LANES:(t + 1) * LANES] - u_b
                         + mask_sc[:, t * LANES:(t + 1) * LANES]) for t in range(nlt)], axis=1)
            qh = qk_ref[rs, h * dk:(h + 1) * dk]
            kh_t = qk_ref[rs, qkw + h * dk:qkw + (h + 1) * dk].T
            vext = jnp.concatenate([v_ref[rs, h * dv:(h + 1) * dv], ones_blk], axis=1)
            s = (_dot(qh, kh_t) * dexp).astype(BF16)
            cext = cext_sc[h]
            hext = (_dot(s, vext) + jnp.concatenate([inter_b, inter_b], axis=1)
                    * _dot(qh, cext.astype(BF16)))
            hm = hext[:, :dv] / jnp.maximum(jnp.abs(hext[:, dv:]), floor_b)
            y = _rms(hm, nw_ref[:, h * dv:(h + 1) * dv])
            y = so_ref[rs, h * dv:(h + 1) * dv].astype(F32) * y
            out_ref[rs, h * dv:(h + 1) * dv] = y.astype(BF16)
            kw_t = (kh_t.astype(F32) * ws_t[h:h + 1, :]).astype(BF16)
            cext_sc[h] = decay[:, h:h + 1] * cext + _dot(kw_t, vext)

    for ci in range(ML_CHUNKS_PER_STEP):
        chunk(ci * L)


def _mlstm(proj, gates, gate_b, norm_w, batch, seq):
    n = proj.shape[0]
    L = ML_CHUNK
    rows = L * ML_CHUNKS_PER_STEP
    nc = seq // rows
    w = ML_HEADS * ML_V_DIM
    return pl.pallas_call(
        _mlstm_kernel,
        grid=(batch, nc),
        in_specs=[
            pl.BlockSpec((rows, PROJ_TN), lambda b, c: (b * nc + c, SEG_MQK)),
            pl.BlockSpec((rows, PROJ_TN), lambda b, c: (b * nc + c, SEG_MV)),
            pl.BlockSpec((rows, PROJ_TN), lambda b, c: (b * nc + c, SEG_MO)),
            pl.BlockSpec((rows, LANES), lambda b, c: (b * nc + c, 0)),
            pl.BlockSpec((1, LANES), lambda b, c: (0, 0)),
            pl.BlockSpec((1, w), lambda b, c: (0, 0)),
        ],
        out_specs=pl.BlockSpec((rows, w), lambda b, c: (b * nc + c, 0)),
        out_shape=jax.ShapeDtypeStruct((n, w), BF16),
        scratch_shapes=[
            pltpu.VMEM((ML_HEADS, ML_QK_DIM, 2 * LANES), F32),
            pltpu.VMEM((SUBLANES, LANES), F32),
            pltpu.VMEM((L, L), F32),
        ],
        compiler_params=pltpu.CompilerParams(
            dimension_semantics=("arbitrary", "arbitrary"),
            vmem_limit_bytes=VMEM_LIMIT_BYTES),
        name="mlstm",
    )(proj, proj, proj, gates, gate_b, norm_w)


def _post_kernel(x_ref, oa_ref, ob_ref, g0_ref, g1_ref, wa_ref, wb_ref, wo_ref, nf_ref,
                 w1_ref, w2_ref, fw_ref, out_ref, *, final_norm):
    ya = _dot(oa_ref[...], wa_ref[...])
    yb = _dot(ob_ref[...], wb_ref[...])
    merged = g0_ref[...].astype(F32) * ya + g1_ref[...].astype(F32) * yb
    h = x_ref[...] + _dot(merged.astype(BF16), wo_ref[...])
    hn = _rms(h, nf_ref[...]).astype(BF16)
    acc = h
    for c0 in range(0, D_FF, FF_CHUNK):
        u = jnp.maximum(_dot(hn, w1_ref[:, c0:c0 + FF_CHUNK]), 0.0)
        acc = acc + _dot((u * u).astype(BF16), w2_ref[c0:c0 + FF_CHUNK, :])
    if final_norm:
        acc = _rms(acc, fw_ref[...])
    out_ref[...] = acc


def _post(x2, oa, ob, proj, wa, wb, wo, nf, w1, w2, fw, final_norm):
    n = x2.shape[0]
    tm = POST_TM
    const = lambda i: (0, 0)
    single = pl.Buffered(1)
    kern = functools.partial(_post_kernel, final_norm=final_norm)
    return pl.pallas_call(
        kern,
        grid=(n // tm,),
        in_specs=[
            pl.BlockSpec((tm, D_MODEL), lambda i: (i, 0)),
            pl.BlockSpec((tm, D_MODEL), lambda i: (i, 0)),
            pl.BlockSpec((tm, D_MODEL), lambda i: (i, 0)),
            pl.BlockSpec((tm, PROJ_TN), lambda i: (i, SEG_G0)),
            pl.BlockSpec((tm, PROJ_TN), lambda i: (i, SEG_G1)),
            pl.BlockSpec((D_MODEL, D_MODEL), const, pipeline_mode=single),
            pl.BlockSpec((D_MODEL, D_MODEL), const, pipeline_mode=single),
            pl.BlockSpec((D_MODEL, D_MODEL), const, pipeline_mode=single),
            pl.BlockSpec((1, D_MODEL), const),
            pl.BlockSpec((D_MODEL, D_FF), const, pipeline_mode=single),
            pl.BlockSpec((D_FF, D_MODEL), const, pipeline_mode=single),
            pl.BlockSpec((1, D_MODEL), const),
        ],
        out_specs=pl.BlockSpec((tm, D_MODEL), lambda i: (i, 0)),
        out_shape=jax.ShapeDtypeStruct((n, D_MODEL), F32),
        compiler_params=pltpu.CompilerParams(
            dimension_semantics=("arbitrary",),
            vmem_limit_bytes=VMEM_LIMIT_BYTES),
        name="post_mixer",
    )(x2, oa, ob, proj, proj, wa, wb, wo, nf, w1, w2, fw)


def kernel(x, positions, norm_mix_w, w_in, ml_gate_b, conv_w, conv_b, da_lambda, da_subln_w,
           ml_norm_w, w_proj_a, w_proj_b, w_out, norm_ffn_w, w_ff1, w_ff2, final_norm_w):
    batch, seq, _ = x.shape
    n = batch * seq
    depth = w_in.shape[0]
    assert seq % ATT_T == 0 and seq % PROJ_TM == 0 and n % POST_TM == 0
    assert seq % (ML_CHUNK * ML_CHUNKS_PER_STEP) == 0

    da_w = DA_HEADS * 2 * DA_HEAD_DIM
    ml_qk = ML_HEADS * ML_QK_DIM
    ml_v = ML_HEADS * ML_V_DIM
    o_mq = 3 * da_w
    o_mv = o_mq + 2 * ml_qk
    o_gi = o_mv + ml_v
    o_mo = o_gi + 2 * ML_HEADS
    o_gate = o_mo + ml_v

    pos2 = positions.reshape(n, 1)
    inv = ROPE_THETA ** (-jnp.arange(0, ROPE_DIM, 2, dtype=F32) / ROPE_DIM)
    invf = jnp.tile(inv, LANES // (ROPE_DIM // 2)).reshape(1, LANES)

    h = x.reshape(n, D_MODEL)
    for l in range(depth):
        lambda_init = 0.8 - 0.6 * math.exp(-0.3 * l)
        w = w_in[l].astype(BF16)
        wp = jnp.concatenate([w[:, :o_gi], w[:, o_mo:]], axis=1)
        wg = jnp.pad(w[:, o_gi:o_mo], ((0, 0), (0, LANES - 2 * ML_HEADS)))
        gate_b = jnp.pad(ml_gate_b[l], (0, LANES - 2 * ML_HEADS)).reshape(1, LANES)

        proj, gates = _inproj(h, pos2, norm_mix_w[l].reshape(1, D_MODEL), invf, conv_w[l],
                              conv_b[l].reshape(1, -1), wp, wg, seq)
        oa = _attention(proj, da_lambda[l], da_subln_w[l].reshape(1, DA_V_DIM), batch, seq,
                        lambda_init)
        ob = _mlstm(proj, gates, gate_b, ml_norm_w[l].reshape(1, -1), batch, seq)
        h = _post(h, oa, ob, proj,
                  w_proj_a[l].astype(BF16), w_proj_b[l].astype(BF16), w_out[l].astype(BF16),
                  norm_ffn_w[l].reshape(1, D_MODEL), w_ff1[l].astype(BF16),
                  w_ff2[l].astype(BF16), final_norm_w.reshape(1, D_MODEL),
                  final_norm=(l == depth - 1))
    return h.reshape(batch, seq, D_MODEL)
```

```python
import functools
import math

import jax
import jax.numpy as jnp
from jax import lax
from jax.experimental import pallas as pl
from jax.experimental.pallas import tpu as pltpu

F32 = jnp.float32
BF16 = jnp.bfloat16

D_MODEL = 1024
DA_HEADS = 8
DA_HEAD_DIM = 64
DA_V_DIM = 128
ROPE_DIM = 16
ROPE_THETA = 500000.0
ML_HEADS = 8
ML_QK_DIM = 64
ML_V_DIM = 128
CONV_WIDTH = 4
D_FF = 4 * D_MODEL
EPS = 1e-6

LANES = 128
SUBLANES = 8
VMEM_LIMIT_BYTES = 56 * 1024 * 1024

PROJ_TM = 512
PROJ_TN = 1024
PROJ_CN = 256
PROJ_TABLE_PIECES = 8
ATT_T = 512
ATT_STEPS_PER_REGION = 9
ML_CHUNK = 256
ML_CHUNKS_PER_STEP = 1
POST_TM = 512
FF_CHUNK = 1024

SEG_Q, SEG_K, SEG_V, SEG_MQK, SEG_MV, SEG_MO, SEG_G0, SEG_G1 = range(8)
N_SEG = 8

NEG_BIG = -1e30
LOG2E = 1.4426950408889634


def _dot(a, b):
    return jnp.dot(a, b, preferred_element_type=F32)


def _dot_nt(a, b):
    return lax.dot_general(a, b, (((1,), (1,)), ((), ())), preferred_element_type=F32)


def _dot_tn(a, b):
    return lax.dot_general(a, b, (((0,), (0,)), ((), ())), preferred_element_type=F32)


def _sigmoid(x):
    return 0.5 * jnp.tanh(0.5 * x) + 0.5


def _rms(x, w):
    return x * lax.rsqrt(jnp.mean(x * x, axis=-1, keepdims=True) + EPS) * w


def _inproj_kernel(x_ref, pos_ref, nw_ref, invf_ref, cw_ref, cb_ref, wa_ref, wb_ref, wg_ref,
                   out_ref, gates_ref, cbuf, *, tiles_per_seq):
    tm = PROJ_TM
    cn = PROJ_CN
    i = pl.program_id(0)

    hist_rows = CONV_WIDTH - 1
    first = i % tiles_per_seq == 0

    @pl.when(first)
    def _():
        cbuf[0:SUBLANES, :] = jnp.zeros((SUBLANES, PROJ_TN), F32)

    @pl.when(jnp.logical_not(first))
    def _():
        cbuf[0:SUBLANES, :] = cbuf[tm:tm + SUBLANES, :]

    xn = _rms(x_ref[...], nw_ref[...]).astype(BF16)
    gates_ref[...] = _dot(xn, wg_ref[...])
    half = ROPE_DIM // 2

    def chunk(seg, c_in_seg, tables):
        c0 = seg * PROJ_TN + c_in_seg
        if seg < SEG_MO:
            w_chunk = wa_ref[:, c0:c0 + cn]
        else:
            w_chunk = wb_ref[:, c0 - SEG_MO * PROJ_TN:c0 - SEG_MO * PROJ_TN + cn]
        acc = _dot(xn, w_chunk)
        if seg in (SEG_Q, SEG_K):
            cos, sina, sinb = tables
            scale = DA_HEAD_DIM ** -0.5 * LOG2E if seg == SEG_Q else 1.0
            for l0 in range(0, cn, LANES):
                xc = acc[:, l0:l0 + LANES]
                r = (xc * cos + pltpu.roll(xc, LANES - half, 1) * sina
                     + pltpu.roll(xc, half, 1) * sinb)
                out_ref[:, c0 + l0:c0 + l0 + LANES] = (r * scale).astype(BF16)
        elif seg == SEG_MQK:
            m0 = c0 - SEG_MQK * PROJ_TN
            cbuf[SUBLANES:SUBLANES + tm, m0:m0 + cn] = acc
            conv = cb_ref[:, m0:m0 + cn]
            for j in range(CONV_WIDTH):
                off = SUBLANES - hist_rows + j
                conv = conv + (cw_ref[j:j + 1, m0:m0 + cn]
                               * cbuf[off:off + tm, m0:m0 + cn])
            y = conv * _sigmoid(conv)
            if m0 < ML_HEADS * ML_QK_DIM:
                y = y * (ML_QK_DIM ** -0.5)
            out_ref[:, c0:c0 + cn] = y.astype(BF16)
        elif seg in (SEG_V, SEG_MV):
            out_ref[:, c0:c0 + cn] = acc.astype(BF16)
        else:
            out_ref[:, c0:c0 + cn] = _sigmoid(acc).astype(BF16)

    lane = lax.broadcasted_iota(jnp.int32, (1, LANES), 1) % DA_HEAD_DIM

    def table_piece(r0, r1):
        ang = pos_ref[r0:r1, :].astype(F32) * invf_ref[...]
        c = jnp.cos(ang)
        s = jnp.sin(ang)
        return (jnp.where(lane < ROPE_DIM, c, 1.0), jnp.where(lane < half, -s, 0.0),
                jnp.where((lane >= half) & (lane < ROPE_DIM), s, 0.0))

    piece_rows = tm // PROJ_TABLE_PIECES
    pieces = []
    for c_in_seg in range(0, PROJ_TN, cn):
        for seg in (SEG_V, SEG_MQK, SEG_MV, SEG_G0, SEG_MO, SEG_G1):
            chunk(seg, c_in_seg, None)
            if len(pieces) < PROJ_TABLE_PIECES:
                r0 = len(pieces) * piece_rows
                pieces.append(table_piece(r0, r0 + piece_rows))
    tables = tuple(jnp.concatenate([p[k] for p in pieces], axis=0) for k in range(3))
    for c_in_seg in range(0, PROJ_TN, cn):
        for seg in (SEG_Q, SEG_K):
            chunk(seg, c_in_seg, tables)


def _inproj(x2, pos2, nw, invf, conv_w, conv_b, wa, wb, wg, seq):
    n = x2.shape[0]
    tm = PROJ_TM
    const = lambda i: (0, 0)
    single = pl.Buffered(1)
    kern = functools.partial(_inproj_kernel, tiles_per_seq=seq // tm)
    return pl.pallas_call(
        kern,
        grid=(n // tm,),
        in_specs=[
            pl.BlockSpec((tm, D_MODEL), lambda i: (i, 0)),
            pl.BlockSpec((tm, 1), lambda i: (i, 0)),
            pl.BlockSpec((1, D_MODEL), const),
            pl.BlockSpec((1, LANES), const),
            pl.BlockSpec((CONV_WIDTH, PROJ_TN), const),
            pl.BlockSpec((1, PROJ_TN), const),
            pl.BlockSpec((D_MODEL, SEG_MO * PROJ_TN), const, pipeline_mode=single),
            pl.BlockSpec((D_MODEL, (N_SEG - SEG_MO) * PROJ_TN), const, pipeline_mode=single),
            pl.BlockSpec((D_MODEL, LANES), const, pipeline_mode=single),
        ],
        out_specs=[
            pl.BlockSpec((tm, N_SEG * PROJ_TN), lambda i: (i, 0)),
            pl.BlockSpec((tm, LANES), lambda i: (i, 0)),
        ],
        out_shape=[
            jax.ShapeDtypeStruct((n, N_SEG * PROJ_TN), BF16),
            jax.ShapeDtypeStruct((n, LANES), F32),
        ],
        scratch_shapes=[
            pltpu.VMEM((tm + SUBLANES, PROJ_TN), F32),
        ],
        compiler_params=pltpu.CompilerParams(
            dimension_semantics=("arbitrary",),
            vmem_limit_bytes=VMEM_LIMIT_BYTES),
        name="inproj",
    )(x2, pos2, nw, invf, conv_w, conv_b, wa, wb, wg)


def _attn_kernel(lam_ref, sw_ref, q_ref, k_ref, v_ref, o_ref, vext_sc, sa_sc, sb_sc, pa_sc,
                 pb_sc, acc_sc, m_sc, *, seq, lambda_init):
    t = ATT_T
    hd = DA_HEAD_DIM
    dv = DA_V_DIM
    nq = seq // t
    rows = 2 * t

    @pl.when((pl.program_id(0) == 0) & (pl.program_id(1) == 0))
    def _():
        vext_sc[:, dv:] = jnp.ones((seq, LANES), BF16)

    vext_sc[:, :dv] = v_ref[...]

    lp = lam_ref[...]
    lam = (jnp.exp(jnp.sum(lp[0:1] * lp[1:2], axis=-1, keepdims=True))
           - jnp.exp(jnp.sum(lp[2:3] * lp[3:4], axis=-1, keepdims=True)) + lambda_init)

    def blk(i):
        return pl.ds(i * t if isinstance(i, int) else pl.multiple_of(i * t, t), t)

    def split_q(qi):
        q = q_ref[blk(qi), :]
        lane = lax.broadcasted_iota(jnp.int32, (t, LANES), 1)
        zero = jnp.zeros_like(q)
        return jnp.where(lane < hd, q, zero), jnp.where(lane >= hd, q, zero)

    def lane_tile_max(s):
        pm = s[:, 0:LANES]
        for c in range(1, s.shape[1] // LANES):
            pm = jnp.maximum(pm, s[:, c * LANES:(c + 1) * LANES])
        return pm

    def exp_tiles(src, r0, r1, width, m):
        return jnp.concatenate(
            [jnp.exp2(src[r0:r1, c * LANES:(c + 1) * LANES] - m).astype(BF16)
             for c in range(width // LANES)], axis=1)

    def scores(dst_s, dst_pm, qi, kblk):
        q0, q1 = split_q(qi)
        s = _dot_nt(jnp.concatenate([q0, q1], axis=0), k_ref[blk(kblk), :])
        dst_s[...] = s
        dst_pm[...] = lane_tile_max(s)

    def process(src_s, src_pm, qi, vblk):
        m_prev = m_sc[qi]
        m_new = jnp.maximum(m_prev, jnp.max(src_pm[...], axis=-1, keepdims=True))
        alpha = jnp.exp2(m_prev - m_new)
        p = exp_tiles(src_s, 0, rows, t, m_new)
        pv = _dot(p, vext_sc[blk(vblk), :])
        acc_sc[qi] = jnp.concatenate([alpha, alpha], axis=1) * acc_sc[qi] + pv
        m_sc[qi] = m_new

    hh = t // 2

    def scores_diag(dst_s, dst_pm, qi):
        q0, q1 = split_q(qi)
        k_all = k_ref[qi * t:(qi + 1) * t, :]
        s_a = _dot_nt(jnp.concatenate([q0[:hh], q1[:hh]], axis=0), k_all[:hh])
        s_b = _dot_nt(jnp.concatenate([q0[hh:], q1[hh:]], axis=0), k_all)
        keep_a = (lax.broadcasted_iota(jnp.int32, (hh, hh), 1)
                  <= lax.broadcasted_iota(jnp.int32, (hh, hh), 0))
        keep_b = (lax.broadcasted_iota(jnp.int32, (hh, t), 1)
                  <= lax.broadcasted_iota(jnp.int32, (hh, t), 0) + hh)
        s_a = jnp.where(jnp.concatenate([keep_a, keep_a], axis=0), s_a, NEG_BIG)
        s_b = jnp.where(jnp.concatenate([keep_b, keep_b], axis=0), s_b, NEG_BIG)
        dst_s[0:t, 0:hh] = s_a
        dst_s[t:rows, :] = s_b
        dst_pm[0:t, :] = lane_tile_max(s_a)
        dst_pm[t:rows, :] = lane_tile_max(s_b)

    def process_diag(src_s, src_pm, qi):
        m_a = jnp.broadcast_to(jnp.max(src_pm[0:t, :], axis=-1, keepdims=True), (t, LANES))
        m_b = jnp.broadcast_to(jnp.max(src_pm[t:rows, :], axis=-1, keepdims=True), (t, LANES))
        pv_a = _dot(exp_tiles(src_s, 0, t, hh, m_a), vext_sc[qi * t:qi * t + hh, :])
        pv_b = _dot(exp_tiles(src_s, t, rows, t, m_b), vext_sc[qi * t:(qi + 1) * t, :])
        for dst, val_acc, val_m in ((0, pv_a[:hh], m_a[:hh]), (hh, pv_b[:hh], m_b[:hh]),
                                    (t, pv_a[hh:], m_a[hh:]), (t + hh, pv_b[hh:], m_b[hh:])):
            acc_sc[qi, dst:dst + hh, :] = val_acc
            m_sc[qi, dst:dst + hh, :] = val_m

    def finalize(qi):
        acc = acc_sc[qi]
        o = acc[:, :dv] / acc[:, dv:]
        od = o[:t] - lam * o[t:]
        y = _rms(od, sw_ref[...]) * (1.0 - lambda_init)
        o_ref[qi * t:(qi + 1) * t, :] = y.astype(BF16)

    bufs = ((sa_sc, pa_sc), (sb_sc, pb_sc))
    steps = []
    for qi in range(nq):
        steps.append((qi, qi, True))
        steps.extend((qi, j, False) for j in range(qi))
    def emit_scores(g):
        qi, kblk, diag = steps[g]
        if diag:
            scores_diag(*bufs[g % 2], qi)
        else:
            scores(*bufs[g % 2], qi, kblk)

    emit_scores(0)

    def emit(g0, g1):
        for g in range(g0, g1):
            qi, kblk, diag = steps[g]
            if g + 1 < len(steps):
                emit_scores(g + 1)
            if diag:
                process_diag(*bufs[g % 2], qi)
            else:
                process(*bufs[g % 2], qi, kblk)
            if g + 1 == len(steps) or steps[g + 1][0] != qi:
                finalize(qi)

    one = jnp.minimum(pl.program_id(0), 0) + 1
    for g0 in range(0, len(steps), ATT_STEPS_PER_REGION):
        g1 = min(g0 + ATT_STEPS_PER_REGION, len(steps))

        def region(_, c, g0=g0, g1=g1):
            emit(g0, g1)
            return c

        lax.fori_loop(0, one, region, 0)


def _attention(proj, lam_p, subln_w, batch, seq, lambda_init):
    n = proj.shape[0]
    kern = functools.partial(_attn_kernel, seq=seq, lambda_init=lambda_init)
    hb = PROJ_TN // LANES
    return pl.pallas_call(
        kern,
        grid=(batch, DA_HEADS),
        in_specs=[
            pl.BlockSpec((4, DA_HEAD_DIM), lambda b, h: (0, 0)),
            pl.BlockSpec((1, DA_V_DIM), lambda b, h: (0, 0)),
            pl.BlockSpec((seq, LANES), lambda b, h: (b, SEG_Q * hb + h)),
            pl.BlockSpec((seq, LANES), lambda b, h: (b, SEG_K * hb + h)),
            pl.BlockSpec((seq, LANES), lambda b, h: (b, SEG_V * hb + h)),
        ],
        out_specs=pl.BlockSpec((seq, LANES), lambda b, h: (b, h)),
        out_shape=jax.ShapeDtypeStruct((n, DA_HEADS * DA_V_DIM), BF16),
        scratch_shapes=[
            pltpu.VMEM((seq, 2 * LANES), BF16),
            pltpu.VMEM((2 * ATT_T, ATT_T), F32),
            pltpu.VMEM((2 * ATT_T, ATT_T), F32),
            pltpu.VMEM((2 * ATT_T, LANES), F32),
            pltpu.VMEM((2 * ATT_T, LANES), F32),
            pltpu.VMEM((seq // ATT_T, 2 * ATT_T, 2 * LANES), F32),
            pltpu.VMEM((seq // ATT_T, 2 * ATT_T, LANES), F32),
        ],
        compiler_params=pltpu.CompilerParams(
            dimension_semantics=("arbitrary", "arbitrary"),
            vmem_limit_bytes=VMEM_LIMIT_BYTES),
        name="diff_attention",
    )(lam_p, subln_w, proj, proj, proj)


def _mlstm_kernel(qk_ref, v_ref, so_ref, g_ref, gb_ref, nw_ref, out_ref, cext_sc, m_sc,
                  mask_sc):
    L = ML_CHUNK
    dk, dv = ML_QK_DIM, ML_V_DIM
    qkw = ML_HEADS * dk
    nlt = L // LANES
    c = pl.program_id(1)

    @pl.when(c == 0)
    def _():
        cext_sc[...] = jnp.zeros(cext_sc.shape, F32)
        m_sc[...] = jnp.zeros(m_sc.shape, F32)
        row = lax.broadcasted_iota(jnp.int32, (L, L), 0)
        col = lax.broadcasted_iota(jnp.int32, (L, L), 1)
        mask_sc[...] = jnp.where(col <= row, 0.0, NEG_BIG)

    tri_b = (mask_sc[...] == 0.0).astype(BF16)
    rowi = lax.broadcasted_iota(jnp.int32, (L, LANES), 0)
    ones_blk = jnp.ones((L, LANES), BF16)

    def chunk(r0):
        rs = slice(r0, r0 + L)
        pre = []
        for h in range(ML_HEADS):
            qh = qk_ref[rs, h * dk:(h + 1) * dk]
            kh_t = qk_ref[rs, qkw + h * dk:qkw + (h + 1) * dk].T
            cext = cext_sc[h]
            pre.append((kh_t, cext, _dot(qh, kh_t), _dot(qh, cext.astype(BF16))))
        g = g_ref[rs, :] + gb_ref[...]
        logf = jnp.minimum(g, 0.0) - jnp.log(1.0 + jnp.exp(-jnp.abs(g)))
        hi = logf.astype(BF16)
        r1 = logf - hi.astype(F32)
        mid = r1.astype(BF16)
        lo = (r1 - mid.astype(F32)).astype(BF16)
        bcs = _dot(tri_b, hi) + _dot(tri_b, mid) + _dot(tri_b, lo)
        b_al = pltpu.roll(bcs, LANES - ML_HEADS, 1)
        a = g - b_al
        m_prev = m_sc[0:1, :]
        cm = a
        d = 1
        while d < L:
            cm = jnp.maximum(cm, jnp.where(rowi >= d, pltpu.roll(cm, d, 0), NEG_BIG))
            d *= 2
        u = jnp.maximum(cm, m_prev)
        mt = b_al + u
        b_last = b_al[L - 1:L, :]
        w_log = b_last + a
        m_new = jnp.maximum(b_last + m_prev, jnp.max(w_log, axis=0, keepdims=True))
        decay = jnp.exp(b_last + m_prev - m_new)
        ws = jnp.exp(w_log - m_new)
        m_sc[0:1, :] = m_new
        a_t = a.T
        ws_t = ws.T

        for h in range(ML_HEADS):
            u_b = jnp.broadcast_to(u[:, h:h + 1], (L, LANES))
            mt_b = jnp.broadcast_to(mt[:, h:h + 1], (L, LANES))
            inter_b = jnp.exp(m_prev[:, h:h + 1] - u_b)
            floor_b = jnp.exp(-mt_b)
            dexp = jnp.concatenate(
                [jnp.exp(a_t[h:h + 1, t * LANES:(t + 1) * LANES] - u_b
                         + mask_sc[:, t * LANES:(t + 1) * LANES]) for t in range(nlt)], axis=1)
            kh_t, cext, qk, qc = pre[h]
            vext = jnp.concatenate([v_ref[rs, h * dv:(h + 1) * dv], ones_blk], axis=1)
            s = (qk * dexp).astype(BF16)
            hext = _dot(s, vext) + jnp.concatenate([inter_b, inter_b], axis=1) * qc
            hm = hext[:, :dv] / jnp.maximum(jnp.abs(hext[:, dv:]), floor_b)
            y = _rms(hm, nw_ref[:, h * dv:(h + 1) * dv])
            y = so_ref[rs, h * dv:(h + 1) * dv].astype(F32) * y
            out_ref[rs, h * dv:(h + 1) * dv] = y.astype(BF16)
            kw_t = (kh_t.astype(F32) * ws_t[h:h + 1, :]).astype(BF16)
            cext_sc[h] = decay[:, h:h + 1] * cext + _dot(kw_t, vext)

    for ci in range(ML_CHUNKS_PER_STEP):
        chunk(ci * L)


def _mlstm(proj, gates, gate_b, norm_w, batch, seq):
    n = proj.shape[0]
    L = ML_CHUNK
    rows = L * ML_CHUNKS_PER_STEP
    nc = seq // rows
    w = ML_HEADS * ML_V_DIM
    return pl.pallas_call(
        _mlstm_kernel,
        grid=(batch, nc),
        in_specs=[
            pl.BlockSpec((rows, PROJ_TN), lambda b, c: (b * nc + c, SEG_MQK)),
            pl.BlockSpec((rows, PROJ_TN), lambda b, c: (b * nc + c, SEG_MV)),
            pl.BlockSpec((rows, PROJ_TN), lambda b, c: (b * nc + c, SEG_MO)),
            pl.BlockSpec((rows, LANES), lambda b, c: (b * nc + c, 0)),
            pl.BlockSpec((1, LANES), lambda b, c: (0, 0)),
            pl.BlockSpec((1, w), lambda b, c: (0, 0)),
        ],
        out_specs=pl.BlockSpec((rows, w), lambda b, c: (b * nc + c, 0)),
        out_shape=jax.ShapeDtypeStruct((n, w), BF16),
        scratch_shapes=[
            pltpu.VMEM((ML_HEADS, ML_QK_DIM, 2 * LANES), F32),
            pltpu.VMEM((SUBLANES, LANES), F32),
            pltpu.VMEM((L, L), F32),
        ],
        compiler_params=pltpu.CompilerParams(
            dimension_semantics=("arbitrary", "arbitrary"),
            vmem_limit_bytes=VMEM_LIMIT_BYTES),
        name="mlstm",
    )(proj, proj, proj, gates, gate_b, norm_w)


def _post_kernel(x_ref, oa_ref, ob_ref, g0_ref, g1_ref, wa_ref, wb_ref, wo_ref, nf_ref,
                 w1_ref, w2_ref, fw_ref, out_ref, *, final_norm):
    ya = _dot(oa_ref[...], wa_ref[...])
    yb = _dot(ob_ref[...], wb_ref[...])
    merged = g0_ref[...].astype(F32) * ya + g1_ref[...].astype(F32) * yb
    h = x_ref[...] + _dot(merged.astype(BF16), wo_ref[...])
    hn = _rms(h, nf_ref[...]).astype(BF16)
    acc = h
    for c0 in range(0, D_FF, FF_CHUNK):
        u = jnp.maximum(_dot(hn, w1_ref[:, c0:c0 + FF_CHUNK]), 0.0)
        acc = acc + _dot((u * u).astype(BF16), w2_ref[c0:c0 + FF_CHUNK, :])
    if final_norm:
        acc = _rms(acc, fw_ref[...])
    out_ref[...] = acc


def _post(x2, oa, ob, proj, wa, wb, wo, nf, w1, w2, fw, final_norm):
    n = x2.shape[0]
    tm = POST_TM
    const = lambda i: (0, 0)
    single = pl.Buffered(1)
    kern = functools.partial(_post_kernel, final_norm=final_norm)
    return pl.pallas_call(
        kern,
        grid=(n // tm,),
        in_specs=[
            pl.BlockSpec((tm, D_MODEL), lambda i: (i, 0)),
            pl.BlockSpec((tm, D_MODEL), lambda i: (i, 0)),
            pl.BlockSpec((tm, D_MODEL), lambda i: (i, 0)),
            pl.BlockSpec((tm, PROJ_TN), lambda i: (i, SEG_G0)),
            pl.BlockSpec((tm, PROJ_TN), lambda i: (i, SEG_G1)),
            pl.BlockSpec((D_MODEL, D_MODEL), const, pipeline_mode=single),
            pl.BlockSpec((D_MODEL, D_MODEL), const, pipeline_mode=single),
            pl.BlockSpec((D_MODEL, D_MODEL), const, pipeline_mode=single),
            pl.BlockSpec((1, D_MODEL), const),
            pl.BlockSpec((D_MODEL, D_FF), const, pipeline_mode=single),
            pl.BlockSpec((D_FF, D_MODEL), const, pipeline_mode=single),
            pl.BlockSpec((1, D_MODEL), const),
        ],
        out_specs=pl.BlockSpec((tm, D_MODEL), lambda i: (i, 0)),
        out_shape=jax.ShapeDtypeStruct((n, D_MODEL), F32),
        compiler_params=pltpu.CompilerParams(
            dimension_semantics=("arbitrary",),
            vmem_limit_bytes=VMEM_LIMIT_BYTES),
        name="post_mixer",
    )(x2, oa, ob, proj, proj, wa, wb, wo, nf, w1, w2, fw)


def kernel(x, positions, norm_mix_w, w_in, ml_gate_b, conv_w, conv_b, da_lambda, da_subln_w,
           ml_norm_w, w_proj_a, w_proj_b, w_out, norm_ffn_w, w_ff1, w_ff2, final_norm_w):
    batch, seq, _ = x.shape
    n = batch * seq
    depth = w_in.shape[0]
    assert seq % ATT_T == 0 and seq % PROJ_TM == 0 and n % POST_TM == 0
    assert seq % (ML_CHUNK * ML_CHUNKS_PER_STEP) == 0

    da_w = DA_HEADS * 2 * DA_HEAD_DIM
    ml_qk = ML_HEADS * ML_QK_DIM
    ml_v = ML_HEADS * ML_V_DIM
    o_mq = 3 * da_w
    o_mv = o_mq + 2 * ml_qk
    o_gi = o_mv + ml_v
    o_mo = o_gi + 2 * ML_HEADS
    o_gate = o_mo + ml_v

    pos2 = positions.reshape(n, 1)
    inv = ROPE_THETA ** (-jnp.arange(0, ROPE_DIM, 2, dtype=F32) / ROPE_DIM)
    invf = jnp.tile(inv, LANES // (ROPE_DIM // 2)).reshape(1, LANES)

    h = x.reshape(n, D_MODEL)
    for l in range(depth):
        lambda_init = 0.8 - 0.6 * math.exp(-0.3 * l)
        w = w_in[l].astype(BF16)
        wa = w[:, :o_gi]
        wb = w[:, o_mo:]
        wg = jnp.pad(w[:, o_gi:o_mo], ((0, 0), (0, LANES - 2 * ML_HEADS)))
        gate_b = jnp.pad(ml_gate_b[l], (0, LANES - 2 * ML_HEADS)).reshape(1, LANES)

        proj, gates = _inproj(h, pos2, norm_mix_w[l].reshape(1, D_MODEL), invf, conv_w[l],
                              conv_b[l].reshape(1, -1), wa, wb, wg, seq)
        oa = _attention(proj, da_lambda[l], da_subln_w[l].reshape(1, DA_V_DIM), batch, seq,
                        lambda_init)
        ob = _mlstm(proj, gates, gate_b, ml_norm_w[l].reshape(1, -1), batch, seq)
        h = _post(h, oa, ob, proj,
                  w_proj_a[l].astype(BF16), w_proj_b[l].astype(BF16), w_out[l].astype(BF16),
                  norm_ffn_w[l].reshape(1, D_MODEL), w_ff1[l].astype(BF16),
                  w_ff2[l].astype(BF16), final_norm_w.reshape(1, D_MODEL),
                  final_norm=(l == depth - 1))
    return h.reshape(batch, seq, D_MODEL)
```

```python
import functools
import math

import jax
import jax.numpy as jnp
from jax import lax
from jax.experimental import pallas as pl
from jax.experimental.pallas import tpu as pltpu

F32 = jnp.float32
BF16 = jnp.bfloat16

D_MODEL = 1024
DA_HEADS = 8
DA_HEAD_DIM = 64
DA_V_DIM = 128
ROPE_DIM = 16
ROPE_THETA = 500000.0
ML_HEADS = 8
ML_QK_DIM = 64
ML_V_DIM = 128
CONV_WIDTH = 4
D_FF = 4 * D_MODEL
EPS = 1e-6

LANES = 128
SUBLANES = 8
VMEM_LIMIT_BYTES = 56 * 1024 * 1024

PROJ_TM = 512
PROJ_TN = 1024
PROJ_CN = 256
PROJ_TABLE_PIECES = 8
ATT_T = 512
ATT_STEPS_PER_REGION = 9
ML_CHUNK = 256
POST_TM = 512
FF_CHUNK = 1024

SEG_Q, SEG_K, SEG_V, SEG_MQK, SEG_MV, SEG_MO, SEG_G0, SEG_G1 = range(8)
N_SEG = 8

NEG_BIG = -1e30
LOG2E = 1.4426950408889634


def _dot(a, b):
    return jnp.dot(a, b, preferred_element_type=F32)


def _dot_nt(a, b):
    return lax.dot_general(a, b, (((1,), (1,)), ((), ())), preferred_element_type=F32)


def _dot_tn(a, b):
    return lax.dot_general(a, b, (((0,), (0,)), ((), ())), preferred_element_type=F32)


def _sigmoid(x):
    return 0.5 * jnp.tanh(0.5 * x) + 0.5


def _rms(x, w):
    return x * lax.rsqrt(jnp.mean(x * x, axis=-1, keepdims=True) + EPS) * w


def _inproj_kernel(x_ref, pos_ref, nw_ref, invf_ref, cw_ref, cb_ref, wa_ref, wb_ref, wg_ref,
                   out_ref, gates_ref, cbuf, xn_sc, *, tiles_per_seq):
    tm = PROJ_TM
    cn = PROJ_CN
    i = pl.program_id(0)

    hist_rows = CONV_WIDTH - 1
    first = i % tiles_per_seq == 0

    @pl.when(first)
    def _():
        cbuf[0:SUBLANES, :] = jnp.zeros((SUBLANES, PROJ_TN), F32)

    @pl.when(jnp.logical_not(first))
    def _():
        cbuf[0:SUBLANES, :] = cbuf[tm:tm + SUBLANES, :]

    xn_sc[...] = _rms(x_ref[...], nw_ref[...]).astype(BF16)
    gates_ref[...] = _dot(xn_sc[...], wg_ref[...])
    half = ROPE_DIM // 2

    def chunk(seg, c_in_seg, tables):
        c0 = seg * PROJ_TN + c_in_seg
        if seg < SEG_MO:
            w_chunk = wa_ref[:, c0:c0 + cn]
        else:
            w_chunk = wb_ref[:, c0 - SEG_MO * PROJ_TN:c0 - SEG_MO * PROJ_TN + cn]
        acc = _dot(xn_sc[...], w_chunk)
        if seg in (SEG_Q, SEG_K):
            cos, sina, sinb = tables
            scale = DA_HEAD_DIM ** -0.5 * LOG2E if seg == SEG_Q else 1.0
            for l0 in range(0, cn, LANES):
                xc = acc[:, l0:l0 + LANES]
                r = (xc * cos + pltpu.roll(xc, LANES - half, 1) * sina
                     + pltpu.roll(xc, half, 1) * sinb)
                out_ref[:, c0 + l0:c0 + l0 + LANES] = (r * scale).astype(BF16)
        elif seg == SEG_MQK:
            m0 = c0 - SEG_MQK * PROJ_TN
            cbuf[SUBLANES:SUBLANES + tm, m0:m0 + cn] = acc
            conv = cb_ref[:, m0:m0 + cn]
            for j in range(CONV_WIDTH):
                off = SUBLANES - hist_rows + j
                conv = conv + (cw_ref[j:j + 1, m0:m0 + cn]
                               * cbuf[off:off + tm, m0:m0 + cn])
            y = conv * _sigmoid(conv)
            if m0 < ML_HEADS * ML_QK_DIM:
                y = y * (ML_QK_DIM ** -0.5)
            out_ref[:, c0:c0 + cn] = y.astype(BF16)
        elif seg in (SEG_V, SEG_MV):
            out_ref[:, c0:c0 + cn] = acc.astype(BF16)
        else:
            out_ref[:, c0:c0 + cn] = _sigmoid(acc).astype(BF16)

    lane = lax.broadcasted_iota(jnp.int32, (1, LANES), 1) % DA_HEAD_DIM

    def table_piece(r0, r1):
        ang = pos_ref[r0:r1, :].astype(F32) * invf_ref[...]
        c = jnp.cos(ang)
        s = jnp.sin(ang)
        return (jnp.where(lane < ROPE_DIM, c, 1.0), jnp.where(lane < half, -s, 0.0),
                jnp.where((lane >= half) & (lane < ROPE_DIM), s, 0.0))

    piece_rows = tm // PROJ_TABLE_PIECES
    pieces = []
    for c_in_seg in range(0, PROJ_TN, cn):
        for seg in (SEG_V, SEG_MQK, SEG_MV, SEG_G0, SEG_MO, SEG_G1):
            chunk(seg, c_in_seg, None)
            if len(pieces) < PROJ_TABLE_PIECES:
                r0 = len(pieces) * piece_rows
                pieces.append(table_piece(r0, r0 + piece_rows))
    tables = tuple(jnp.concatenate([p[k] for p in pieces], axis=0) for k in range(3))
    for c_in_seg in range(0, PROJ_TN, cn):
        for seg in (SEG_Q, SEG_K):
            chunk(seg, c_in_seg, tables)


def _inproj(x2, pos2, nw, invf, conv_w, conv_b, wa, wb, wg, seq):
    n = x2.shape[0]
    tm = PROJ_TM
    const = lambda i: (0, 0)
    single = pl.Buffered(1)
    kern = functools.partial(_inproj_kernel, tiles_per_seq=seq // tm)
    return pl.pallas_call(
        kern,
        grid=(n // tm,),
        in_specs=[
            pl.BlockSpec((tm, D_MODEL), lambda i: (i, 0)),
            pl.BlockSpec((tm, 1), lambda i: (i, 0)),
            pl.BlockSpec((1, D_MODEL), const),
            pl.BlockSpec((1, LANES), const),
            pl.BlockSpec((CONV_WIDTH, PROJ_TN), const),
            pl.BlockSpec((1, PROJ_TN), const),
            pl.BlockSpec((D_MODEL, SEG_MO * PROJ_TN), const, pipeline_mode=single),
            pl.BlockSpec((D_MODEL, (N_SEG - SEG_MO) * PROJ_TN), const, pipeline_mode=single),
            pl.BlockSpec((D_MODEL, LANES), const, pipeline_mode=single),
        ],
        out_specs=[
            pl.BlockSpec((tm, N_SEG * PROJ_TN), lambda i: (i, 0)),
            pl.BlockSpec((tm, LANES), lambda i: (i, 0)),
        ],
        out_shape=[
            jax.ShapeDtypeStruct((n, N_SEG * PROJ_TN), BF16),
            jax.ShapeDtypeStruct((n, LANES), F32),
        ],
        scratch_shapes=[
            pltpu.VMEM((tm + SUBLANES, PROJ_TN), F32),
            pltpu.VMEM((tm, D_MODEL), BF16),
        ],
        compiler_params=pltpu.CompilerParams(
            dimension_semantics=("arbitrary",),
            vmem_limit_bytes=VMEM_LIMIT_BYTES),
        name="inproj",
    )(x2, pos2, nw, invf, conv_w, conv_b, wa, wb, wg)


def _mixers_kernel(lam_ref, sw_ref, q_ref, k_ref, v_ref, mqk_ref, mv_ref, mso_ref, mg_ref,
                   mgb_ref, mnw_ref, o_ref, ob_ref, vext_sc, sa_sc, sb_sc, pa_sc, pb_sc, acc_sc,
                   m_sc, cext_sc, mm_sc, mask_sc, *, seq, lambda_init):
    ml_init, ml_gate, ml_head = _mlstm_parts(mqk_ref, mv_ref, mso_ref, mg_ref, mgb_ref, mnw_ref,
                                             ob_ref, cext_sc, mm_sc, mask_sc)

    @pl.when(pl.program_id(1) == 0)
    def _():
        ml_init()

    t = ATT_T
    hd = DA_HEAD_DIM
    dv = DA_V_DIM
    nq = seq // t
    rows = 2 * t

    @pl.when((pl.program_id(0) == 0) & (pl.program_id(1) == 0))
    def _():
        vext_sc[:, dv:] = jnp.ones((seq, LANES), BF16)

    vext_sc[:, :dv] = v_ref[...]

    lp = lam_ref[...]
    lam = (jnp.exp(jnp.sum(lp[0:1] * lp[1:2], axis=-1, keepdims=True))
           - jnp.exp(jnp.sum(lp[2:3] * lp[3:4], axis=-1, keepdims=True)) + lambda_init)

    def blk(i):
        return pl.ds(i * t if isinstance(i, int) else pl.multiple_of(i * t, t), t)

    def split_q(qi):
        q = q_ref[blk(qi), :]
        lane = lax.broadcasted_iota(jnp.int32, (t, LANES), 1)
        zero = jnp.zeros_like(q)
        return jnp.where(lane < hd, q, zero), jnp.where(lane >= hd, q, zero)

    def lane_tile_max(s):
        pm = s[:, 0:LANES]
        for c in range(1, s.shape[1] // LANES):
            pm = jnp.maximum(pm, s[:, c * LANES:(c + 1) * LANES])
        return pm

    def exp_tiles(src, r0, r1, width, m):
        return jnp.concatenate(
            [jnp.exp2(src[r0:r1, c * LANES:(c + 1) * LANES] - m).astype(BF16)
             for c in range(width // LANES)], axis=1)

    def scores(dst_s, dst_pm, qi, kblk):
        q0, q1 = split_q(qi)
        s = _dot_nt(jnp.concatenate([q0, q1], axis=0), k_ref[blk(kblk), :])
        dst_s[...] = s
        dst_pm[...] = lane_tile_max(s)

    def process(src_s, src_pm, qi, vblk):
        m_prev = m_sc[qi]
        m_new = jnp.maximum(m_prev, jnp.max(src_pm[...], axis=-1, keepdims=True))
        alpha = jnp.exp2(m_prev - m_new)
        p = exp_tiles(src_s, 0, rows, t, m_new)
        pv = _dot(p, vext_sc[blk(vblk), :])
        acc_sc[qi] = jnp.concatenate([alpha, alpha], axis=1) * acc_sc[qi] + pv
        m_sc[qi] = m_new

    hh = t // 2

    def scores_diag(dst_s, dst_pm, qi):
        q0, q1 = split_q(qi)
        k_all = k_ref[qi * t:(qi + 1) * t, :]
        s_a = _dot_nt(jnp.concatenate([q0[:hh], q1[:hh]], axis=0), k_all[:hh])
        s_b = _dot_nt(jnp.concatenate([q0[hh:], q1[hh:]], axis=0), k_all)
        keep_a = (lax.broadcasted_iota(jnp.int32, (hh, hh), 1)
                  <= lax.broadcasted_iota(jnp.int32, (hh, hh), 0))
        keep_b = (lax.broadcasted_iota(jnp.int32, (hh, t), 1)
                  <= lax.broadcasted_iota(jnp.int32, (hh, t), 0) + hh)
        s_a = jnp.where(jnp.concatenate([keep_a, keep_a], axis=0), s_a, NEG_BIG)
        s_b = jnp.where(jnp.concatenate([keep_b, keep_b], axis=0), s_b, NEG_BIG)
        dst_s[0:t, 0:hh] = s_a
        dst_s[t:rows, :] = s_b
        dst_pm[0:t, :] = lane_tile_max(s_a)
        dst_pm[t:rows, :] = lane_tile_max(s_b)

    def process_diag(src_s, src_pm, qi):
        m_a = jnp.broadcast_to(jnp.max(src_pm[0:t, :], axis=-1, keepdims=True), (t, LANES))
        m_b = jnp.broadcast_to(jnp.max(src_pm[t:rows, :], axis=-1, keepdims=True), (t, LANES))
        pv_a = _dot(exp_tiles(src_s, 0, t, hh, m_a), vext_sc[qi * t:qi * t + hh, :])
        pv_b = _dot(exp_tiles(src_s, t, rows, t, m_b), vext_sc[qi * t:(qi + 1) * t, :])
        for dst, val_acc, val_m in ((0, pv_a[:hh], m_a[:hh]), (hh, pv_b[:hh], m_b[:hh]),
                                    (t, pv_a[hh:], m_a[hh:]), (t + hh, pv_b[hh:], m_b[hh:])):
            acc_sc[qi, dst:dst + hh, :] = val_acc
            m_sc[qi, dst:dst + hh, :] = val_m

    def finalize(qi):
        acc = acc_sc[qi]
        o = acc[:, :dv] / acc[:, dv:]
        od = o[:t] - lam * o[t:]
        y = _rms(od, sw_ref[...]) * (1.0 - lambda_init)
        o_ref[qi * t:(qi + 1) * t, :] = y.astype(BF16)

    bufs = ((sa_sc, pa_sc), (sb_sc, pb_sc))
    steps = []
    for qi in range(nq):
        steps.append((qi, qi, True))
        steps.extend((qi, j, False) for j in range(qi))
    def emit_scores(g):
        qi, kblk, diag = steps[g]
        if diag:
            scores_diag(*bufs[g % 2], qi)
        else:
            scores(*bufs[g % 2], qi, kblk)

    emit_scores(0)

    def emit(g0, g1, ml_chunk):
        ctx = None
        heads_done = 0
        for g in range(g0, g1):
            if ml_chunk is not None:
                if ctx is None:
                    ctx = ml_gate(ml_chunk * ML_CHUNK)
                elif heads_done < ML_HEADS:
                    ml_head(ctx, heads_done)
                    heads_done += 1
            qi, kblk, diag = steps[g]
            if g + 1 < len(steps):
                emit_scores(g + 1)
            if diag:
                process_diag(*bufs[g % 2], qi)
            else:
                process(*bufs[g % 2], qi, kblk)
            if g + 1 == len(steps) or steps[g + 1][0] != qi:
                finalize(qi)
        if ml_chunk is not None:
            for h in range(heads_done, ML_HEADS):
                ml_head(ctx, h)

    one = jnp.minimum(pl.program_id(0), 0) + 1
    starts = list(range(0, len(steps), ATT_STEPS_PER_REGION))
    ml_chunks = t // ML_CHUNK
    assert len(starts) % ml_chunks == 0
    for r, g0 in enumerate(starts):
        g1 = min(g0 + ATT_STEPS_PER_REGION, len(steps))
        stride = len(starts) // ml_chunks
        ml_chunk = r // stride if r % stride == 0 else None

        def region(_, c, g0=g0, g1=g1, ml_chunk=ml_chunk):
            emit(g0, g1, ml_chunk)
            return c

        lax.fori_loop(0, one, region, 0)


def _mixers(proj, gates, lam_p, subln_w, gate_b, ml_norm_w, batch, seq, lambda_init):
    n = proj.shape[0]
    t = ATT_T
    nq = seq // t
    assert nq == DA_HEADS and t % ML_CHUNK == 0
    kern = functools.partial(_mixers_kernel, seq=seq, lambda_init=lambda_init)
    hb = PROJ_TN // LANES
    mw = ML_HEADS * ML_V_DIM
    const = lambda b, h: (0, 0)
    return pl.pallas_call(
        kern,
        grid=(batch, DA_HEADS),
        in_specs=[
            pl.BlockSpec((4, DA_HEAD_DIM), const),
            pl.BlockSpec((1, DA_V_DIM), const),
            pl.BlockSpec((seq, LANES), lambda b, h: (b, SEG_Q * hb + h)),
            pl.BlockSpec((seq, LANES), lambda b, h: (b, SEG_K * hb + h)),
            pl.BlockSpec((seq, LANES), lambda b, h: (b, SEG_V * hb + h)),
            pl.BlockSpec((t, PROJ_TN), lambda b, h: (b * nq + h, SEG_MQK)),
            pl.BlockSpec((t, PROJ_TN), lambda b, h: (b * nq + h, SEG_MV)),
            pl.BlockSpec((t, PROJ_TN), lambda b, h: (b * nq + h, SEG_MO)),
            pl.BlockSpec((t, LANES), lambda b, h: (b * nq + h, 0)),
            pl.BlockSpec((1, LANES), const),
            pl.BlockSpec((1, mw), const),
        ],
        out_specs=[
            pl.BlockSpec((seq, LANES), lambda b, h: (b, h)),
            pl.BlockSpec((t, mw), lambda b, h: (b * nq + h, 0)),
        ],
        out_shape=[
            jax.ShapeDtypeStruct((n, DA_HEADS * DA_V_DIM), BF16),
            jax.ShapeDtypeStruct((n, mw), BF16),
        ],
        scratch_shapes=[
            pltpu.VMEM((seq, 2 * LANES), BF16),
            pltpu.VMEM((2 * t, t), F32),
            pltpu.VMEM((2 * t, t), F32),
            pltpu.VMEM((2 * t, LANES), F32),
            pltpu.VMEM((2 * t, LANES), F32),
            pltpu.VMEM((nq, 2 * t, 2 * LANES), F32),
            pltpu.VMEM((nq, 2 * t, LANES), F32),
            pltpu.VMEM((ML_HEADS, ML_QK_DIM, 2 * LANES), F32),
            pltpu.VMEM((SUBLANES, LANES), F32),
            pltpu.VMEM((ML_CHUNK, ML_CHUNK), F32),
        ],
        compiler_params=pltpu.CompilerParams(
            dimension_semantics=("arbitrary", "arbitrary"),
            vmem_limit_bytes=VMEM_LIMIT_BYTES),
        name="mixers",
    )(lam_p, subln_w, proj, proj, proj, proj, proj, proj, gates, gate_b, ml_norm_w)


def _mlstm_parts(qk_ref, v_ref, so_ref, g_ref, gb_ref, nw_ref, out_ref, cext_sc, m_sc, mask_sc):
    L = ML_CHUNK
    dk, dv = ML_QK_DIM, ML_V_DIM
    qkw = ML_HEADS * dk
    nlt = L // LANES

    def init():
        cext_sc[...] = jnp.zeros(cext_sc.shape, F32)
        m_sc[...] = jnp.zeros(m_sc.shape, F32)
        row = lax.broadcasted_iota(jnp.int32, (L, L), 0)
        col = lax.broadcasted_iota(jnp.int32, (L, L), 1)
        mask_sc[...] = jnp.where(col <= row, 0.0, NEG_BIG)

    def gate(r0):
        rs = slice(r0, r0 + L)
        tri_b = (mask_sc[...] == 0.0).astype(BF16)
        rowi = lax.broadcasted_iota(jnp.int32, (L, LANES), 0)
        g = g_ref[rs, :] + gb_ref[...]
        logf = jnp.minimum(g, 0.0) - jnp.log(1.0 + jnp.exp(-jnp.abs(g)))
        hi = logf.astype(BF16)
        r1 = logf - hi.astype(F32)
        mid = r1.astype(BF16)
        lo = (r1 - mid.astype(F32)).astype(BF16)
        bcs = _dot(tri_b, hi) + _dot(tri_b, mid) + _dot(tri_b, lo)
        b_al = pltpu.roll(bcs, LANES - ML_HEADS, 1)
        a = g - b_al
        m_prev = m_sc[0:1, :]
        cm = a
        d = 1
        while d < L:
            cm = jnp.maximum(cm, jnp.where(rowi >= d, pltpu.roll(cm, d, 0), NEG_BIG))
            d *= 2
        u = jnp.maximum(cm, m_prev)
        b_last = b_al[L - 1:L, :]
        w_log = b_last + a
        m_new = jnp.maximum(b_last + m_prev, jnp.max(w_log, axis=0, keepdims=True))
        m_sc[0:1, :] = m_new
        return dict(rs=rs, u=u, mt=b_al + u, m_prev=m_prev,
                    decay=jnp.exp(b_last + m_prev - m_new),
                    a_t=a.T,
                    ws_t=jnp.exp(w_log - m_new).T)

    def head(ctx, h):
        rs = ctx["rs"]
        u_b = jnp.broadcast_to(ctx["u"][:, h:h + 1], (L, LANES))
        mt_b = jnp.broadcast_to(ctx["mt"][:, h:h + 1], (L, LANES))
        inter_b = jnp.exp(ctx["m_prev"][:, h:h + 1] - u_b)
        floor_b = jnp.exp(-mt_b)
        dexp = jnp.concatenate(
            [jnp.exp(ctx["a_t"][h:h + 1, t * LANES:(t + 1) * LANES] - u_b
                     + mask_sc[:, t * LANES:(t + 1) * LANES]) for t in range(nlt)], axis=1)
        qh = qk_ref[rs, h * dk:(h + 1) * dk]
        kh_t = qk_ref[rs, qkw + h * dk:qkw + (h + 1) * dk].T
        cext = cext_sc[h]
        vext = jnp.concatenate([v_ref[rs, h * dv:(h + 1) * dv], jnp.ones((L, LANES), BF16)],
                               axis=1)
        s = (_dot(qh, kh_t) * dexp).astype(BF16)
        hext = (_dot(s, vext)
                + jnp.concatenate([inter_b, inter_b], axis=1) * _dot(qh, cext.astype(BF16)))
        hm = hext[:, :dv] / jnp.maximum(jnp.abs(hext[:, dv:]), floor_b)
        y = _rms(hm, nw_ref[:, h * dv:(h + 1) * dv])
        y = so_ref[rs, h * dv:(h + 1) * dv].astype(F32) * y
        out_ref[rs, h * dv:(h + 1) * dv] = y.astype(BF16)
        kw_t = (kh_t.astype(F32) * ctx["ws_t"][h:h + 1, :]).astype(BF16)
        cext_sc[h] = ctx["decay"][:, h:h + 1] * cext + _dot(kw_t, vext)

    return init, gate, head


def _post_kernel(x_ref, oa_ref, ob_ref, g0_ref, g1_ref, wa_ref, wb_ref, wo_ref, nf_ref,
                 w1_ref, w2_ref, fw_ref, out_ref, *, final_norm):
    ya = _dot(oa_ref[...], wa_ref[...])
    yb = _dot(ob_ref[...], wb_ref[...])
    merged = g0_ref[...].astype(F32) * ya + g1_ref[...].astype(F32) * yb
    h = x_ref[...] + _dot(merged.astype(BF16), wo_ref[...])
    hn = _rms(h, nf_ref[...]).astype(BF16)
    acc = h
    for c0 in range(0, D_FF, FF_CHUNK):
        u = jnp.maximum(_dot(hn, w1_ref[:, c0:c0 + FF_CHUNK]), 0.0)
        acc = acc + _dot((u * u).astype(BF16), w2_ref[c0:c0 + FF_CHUNK, :])
    if final_norm:
        acc = _rms(acc, fw_ref[...])
    out_ref[...] = acc


def _post(x2, oa, ob, proj, wa, wb, wo, nf, w1, w2, fw, final_norm):
    n = x2.shape[0]
    tm = POST_TM
    const = lambda i: (0, 0)
    single = pl.Buffered(1)
    kern = functools.partial(_post_kernel, final_norm=final_norm)
    return pl.pallas_call(
        kern,
        grid=(n // tm,),
        in_specs=[
            pl.BlockSpec((tm, D_MODEL), lambda i: (i, 0)),
            pl.BlockSpec((tm, D_MODEL), lambda i: (i, 0)),
            pl.BlockSpec((tm, D_MODEL), lambda i: (i, 0)),
            pl.BlockSpec((tm, PROJ_TN), lambda i: (i, SEG_G0)),
            pl.BlockSpec((tm, PROJ_TN), lambda i: (i, SEG_G1)),
            pl.BlockSpec((D_MODEL, D_MODEL), const, pipeline_mode=single),
            pl.BlockSpec((D_MODEL, D_MODEL), const, pipeline_mode=single),
            pl.BlockSpec((D_MODEL, D_MODEL), const, pipeline_mode=single),
            pl.BlockSpec((1, D_MODEL), const),
            pl.BlockSpec((D_MODEL, D_FF), const, pipeline_mode=single),
            pl.BlockSpec((D_FF, D_MODEL), const, pipeline_mode=single),
            pl.BlockSpec((1, D_MODEL), const),
        ],
        out_specs=pl.BlockSpec((tm, D_MODEL), lambda i: (i, 0)),
        out_shape=jax.ShapeDtypeStruct((n, D_MODEL), F32),
        compiler_params=pltpu.CompilerParams(
            dimension_semantics=("arbitrary",),
            vmem_limit_bytes=VMEM_LIMIT_BYTES),
        name="post_mixer",
    )(x2, oa, ob, proj, proj, wa, wb, wo, nf, w1, w2, fw)


def kernel(x, positions, norm_mix_w, w_in, ml_gate_b, conv_w, conv_b, da_lambda, da_subln_w,
           ml_norm_w, w_proj_a, w_proj_b, w_out, norm_ffn_w, w_ff1, w_ff2, final_norm_w):
    batch, seq, _ = x.shape
    n = batch * seq
    depth = w_in.shape[0]
    assert seq % ATT_T == 0 and seq % PROJ_TM == 0 and n % POST_TM == 0

    da_w = DA_HEADS * 2 * DA_HEAD_DIM
    ml_qk = ML_HEADS * ML_QK_DIM
    ml_v = ML_HEADS * ML_V_DIM
    o_mq = 3 * da_w
    o_mv = o_mq + 2 * ml_qk
    o_gi = o_mv + ml_v
    o_mo = o_gi + 2 * ML_HEADS
    o_gate = o_mo + ml_v

    pos2 = positions.reshape(n, 1)
    inv = ROPE_THETA ** (-jnp.arange(0, ROPE_DIM, 2, dtype=F32) / ROPE_DIM)
    invf = jnp.tile(inv, LANES // (ROPE_DIM // 2)).reshape(1, LANES)

    h = x.reshape(n, D_MODEL)
    for l in range(depth):
        lambda_init = 0.8 - 0.6 * math.exp(-0.3 * l)
        w = w_in[l].astype(BF16)
        wa = w
        wb = w[:, o_mo:]
        wg = jnp.pad(w[:, o_gi:o_mo], ((0, 0), (0, LANES - 2 * ML_HEADS)))
        gate_b = jnp.pad(ml_gate_b[l], (0, LANES - 2 * ML_HEADS)).reshape(1, LANES)

        proj, gates = _inproj(h, pos2, norm_mix_w[l].reshape(1, D_MODEL), invf, conv_w[l],
                              conv_b[l].reshape(1, -1), wa, wb, wg, seq)
        oa, ob = _mixers(proj, gates, da_lambda[l], da_subln_w[l].reshape(1, DA_V_DIM), gate_b,
                         ml_norm_w[l].reshape(1, -1), batch, seq, lambda_init)
        h = _post(h, oa, ob, proj,
                  w_proj_a[l].astype(BF16), w_proj_b[l].astype(BF16), w_out[l].astype(BF16),
                  norm_ffn_w[l].reshape(1, D_MODEL), w_ff1[l].astype(BF16),
                  w_ff2[l].astype(BF16), final_norm_w.reshape(1, D_MODEL),
                  final_norm=(l == depth - 1))
    return h.reshape(batch, seq, D_MODEL)
```

```python
import functools
import math

import jax
import jax.numpy as jnp
from jax import lax
from jax.experimental import pallas as pl
from jax.experimental.pallas import tpu as pltpu

F32 = jnp.float32
BF16 = jnp.bfloat16

D_MODEL = 1024
DA_HEADS = 8
DA_HEAD_DIM = 64
DA_V_DIM = 128
ROPE_DIM = 16
ROPE_THETA = 500000.0
ML_HEADS = 8
ML_QK_DIM = 64
ML_V_DIM = 128
CONV_WIDTH = 4
D_FF = 4 * D_MODEL
EPS = 1e-6

LANES = 128
SUBLANES = 8
VMEM_LIMIT_BYTES = 56 * 1024 * 1024

PROJ_TM = 512
PROJ_TN = 1024
PROJ_CN = 256
PROJ_TABLE_PIECES = 8
ATT_T = 512
ATT_STEPS_PER_REGION = 18
ML_CHUNK = 256
POST_TM = 512
FF_CHUNK = 1024

SEG_Q, SEG_K, SEG_V, SEG_MQK, SEG_MV, SEG_MO, SEG_G0, SEG_G1 = range(8)
N_SEG = 8

NEG_BIG = -1e30
LOG2E = 1.4426950408889634


def _dot(a, b):
    return jnp.dot(a, b, preferred_element_type=F32)


def _dot_nt(a, b):
    return lax.dot_general(a, b, (((1,), (1,)), ((), ())), preferred_element_type=F32)


def _dot_tn(a, b):
    return lax.dot_general(a, b, (((0,), (0,)), ((), ())), preferred_element_type=F32)


def _sigmoid(x):
    return 0.5 * jnp.tanh(0.5 * x) + 0.5


def _rms(x, w):
    return x * lax.rsqrt(jnp.mean(x * x, axis=-1, keepdims=True) + EPS) * w


def _inproj_kernel(x_ref, pos_ref, nw_ref, invf_ref, cw_ref, cb_ref, wa_ref, wb_ref, wg_ref,
                   out_ref, gates_ref, cbuf, xn_sc, *, tiles_per_seq):
    tm = PROJ_TM
    cn = PROJ_CN
    i = pl.program_id(0)

    hist_rows = CONV_WIDTH - 1
    first = i % tiles_per_seq == 0

    @pl.when(first)
    def _():
        cbuf[0:SUBLANES, :] = jnp.zeros((SUBLANES, PROJ_TN), F32)

    @pl.when(jnp.logical_not(first))
    def _():
        cbuf[0:SUBLANES, :] = cbuf[tm:tm + SUBLANES, :]

    xn_sc[...] = _rms(x_ref[...], nw_ref[...]).astype(BF16)
    gates_ref[...] = _dot(xn_sc[...], wg_ref[...])
    half = ROPE_DIM // 2

    def chunk(seg, c_in_seg, tables):
        c0 = seg * PROJ_TN + c_in_seg
        if seg < SEG_MO:
            w_chunk = wa_ref[:, c0:c0 + cn]
        else:
            w_chunk = wb_ref[:, c0 - SEG_MO * PROJ_TN:c0 - SEG_MO * PROJ_TN + cn]
        acc = _dot(xn_sc[...], w_chunk)
        if seg in (SEG_Q, SEG_K):
            cos, sin_signed, partner_lane = tables
            scale = DA_HEAD_DIM ** -0.5 * LOG2E if seg == SEG_Q else 1.0
            for l0 in range(0, cn, LANES):
                xc = acc[:, l0:l0 + LANES]
                r = xc * cos + jnp.take_along_axis(xc, partner_lane, axis=1) * sin_signed
                out_ref[:, c0 + l0:c0 + l0 + LANES] = (r * scale).astype(BF16)
        elif seg == SEG_MQK:
            m0 = c0 - SEG_MQK * PROJ_TN
            cbuf[SUBLANES:SUBLANES + tm, m0:m0 + cn] = acc
            conv = cb_ref[:, m0:m0 + cn]
            for j in range(CONV_WIDTH):
                off = SUBLANES - hist_rows + j
                conv = conv + (cw_ref[j:j + 1, m0:m0 + cn]
                               * cbuf[off:off + tm, m0:m0 + cn])
            y = conv * _sigmoid(conv)
            if m0 < ML_HEADS * ML_QK_DIM:
                y = y * (ML_QK_DIM ** -0.5)
            out_ref[:, c0:c0 + cn] = y.astype(BF16)
        elif seg in (SEG_V, SEG_MV):
            out_ref[:, c0:c0 + cn] = acc.astype(BF16)
        else:
            out_ref[:, c0:c0 + cn] = _sigmoid(acc).astype(BF16)

    lane = lax.broadcasted_iota(jnp.int32, (1, LANES), 1) % DA_HEAD_DIM

    def table_piece(r0, r1):
        ang = pos_ref[r0:r1, :].astype(F32) * invf_ref[...]
        c = jnp.cos(ang)
        s = jnp.sin(ang)
        return (jnp.where(lane < ROPE_DIM, c, 1.0),
                jnp.where(lane < half, -s, jnp.where(lane < ROPE_DIM, s, 0.0)))

    piece_rows = tm // PROJ_TABLE_PIECES
    pieces = []
    for c_in_seg in range(0, PROJ_TN, cn):
        for seg in (SEG_V, SEG_MQK, SEG_MV, SEG_G0, SEG_MO, SEG_G1):
            chunk(seg, c_in_seg, None)
            if len(pieces) < PROJ_TABLE_PIECES:
                r0 = len(pieces) * piece_rows
                pieces.append(table_piece(r0, r0 + piece_rows))
    lane_id = lax.broadcasted_iota(jnp.int32, (tm, LANES), 1)
    sub = lane_id % DA_HEAD_DIM
    partner_lane = jnp.where(sub < half, lane_id + half,
                             jnp.where(sub < ROPE_DIM, lane_id - half, lane_id))
    tables = tuple(jnp.concatenate([p[k] for p in pieces], axis=0) for k in range(2))
    tables = tables + (partner_lane,)
    for c_in_seg in range(0, PROJ_TN, cn):
        for seg in (SEG_Q, SEG_K):
            chunk(seg, c_in_seg, tables)


def _inproj(x2, pos2, nw, invf, conv_w, conv_b, wa, wb, wg, seq):
    n = x2.shape[0]
    tm = PROJ_TM
    const = lambda i: (0, 0)
    single = pl.Buffered(1)
    kern = functools.partial(_inproj_kernel, tiles_per_seq=seq // tm)
    return pl.pallas_call(
        kern,
        grid=(n // tm,),
        in_specs=[
            pl.BlockSpec((tm, D_MODEL), lambda i: (i, 0)),
            pl.BlockSpec((tm, 1), lambda i: (i, 0)),
            pl.BlockSpec((1, D_MODEL), const),
            pl.BlockSpec((1, LANES), const),
            pl.BlockSpec((CONV_WIDTH, PROJ_TN), const),
            pl.BlockSpec((1, PROJ_TN), const),
            pl.BlockSpec((D_MODEL, SEG_MO * PROJ_TN), const, pipeline_mode=single),
            pl.BlockSpec((D_MODEL, (N_SEG - SEG_MO) * PROJ_TN), const, pipeline_mode=single),
            pl.BlockSpec((D_MODEL, LANES), const, pipeline_mode=single),
        ],
        out_specs=[
            pl.BlockSpec((tm, N_SEG * PROJ_TN), lambda i: (i, 0)),
            pl.BlockSpec((tm, LANES), lambda i: (i, 0)),
        ],
        out_shape=[
            jax.ShapeDtypeStruct((n, N_SEG * PROJ_TN), BF16),
            jax.ShapeDtypeStruct((n, LANES), F32),
        ],
        scratch_shapes=[
            pltpu.VMEM((tm + SUBLANES, PROJ_TN), F32),
            pltpu.VMEM((tm, D_MODEL), BF16),
        ],
        compiler_params=pltpu.CompilerParams(
            dimension_semantics=("arbitrary",),
            vmem_limit_bytes=VMEM_LIMIT_BYTES),
        name="inproj",
    )(x2, pos2, nw, invf, conv_w, conv_b, wa, wb, wg)


def _mixers_kernel(lam_ref, sw_ref, q_ref, k_ref, v_ref, mqk_ref, mv_ref, mso_ref, mg_ref,
                   mgb_ref, mnw_ref, o_ref, ob_ref, vext_sc, sa_sc, sb_sc, pa_sc, pb_sc, acc_sc,
                   m_sc, cext_sc, mm_sc, mask_sc, *, seq, lambda_init):
    ml_init, ml_gate, ml_head = _mlstm_parts(mqk_ref, mv_ref, mso_ref, mg_ref, mgb_ref, mnw_ref,
                                             ob_ref, cext_sc, mm_sc, mask_sc)

    @pl.when(pl.program_id(1) == 0)
    def _():
        ml_init()

    t = ATT_T
    hd = DA_HEAD_DIM
    dv = DA_V_DIM
    nq = seq // t
    rows = 2 * t

    @pl.when((pl.program_id(0) == 0) & (pl.program_id(1) == 0))
    def _():
        vext_sc[:, dv:] = jnp.ones((seq, LANES), BF16)

    vext_sc[:, :dv] = v_ref[...]

    lp = lam_ref[...]
    lam = (jnp.exp(jnp.sum(lp[0:1] * lp[1:2], axis=-1, keepdims=True))
           - jnp.exp(jnp.sum(lp[2:3] * lp[3:4], axis=-1, keepdims=True)) + lambda_init)

    def blk(i):
        return pl.ds(i * t if isinstance(i, int) else pl.multiple_of(i * t, t), t)

    def split_q(qi):
        q = q_ref[blk(qi), :]
        lane = lax.broadcasted_iota(jnp.int32, (t, LANES), 1)
        zero = jnp.zeros_like(q)
        return jnp.where(lane < hd, q, zero), jnp.where(lane >= hd, q, zero)

    def lane_tile_max(s):
        pm = s[:, 0:LANES]
        for c in range(1, s.shape[1] // LANES):
            pm = jnp.maximum(pm, s[:, c * LANES:(c + 1) * LANES])
        return pm

    def exp_tiles(src, r0, r1, width, m):
        return jnp.concatenate(
            [jnp.exp2(src[r0:r1, c * LANES:(c + 1) * LANES] - m).astype(BF16)
             for c in range(width // LANES)], axis=1)

    def scores(dst_s, dst_pm, qi, kblk):
        q0, q1 = split_q(qi)
        s = _dot_nt(jnp.concatenate([q0, q1], axis=0), k_ref[blk(kblk), :])
        dst_s[...] = s
        dst_pm[...] = lane_tile_max(s)

    def process(src_s, src_pm, qi, vblk):
        m_prev = m_sc[qi]
        m_new = jnp.maximum(m_prev, jnp.max(src_pm[...], axis=-1, keepdims=True))
        alpha = jnp.exp2(m_prev - m_new)
        p = exp_tiles(src_s, 0, rows, t, m_new)
        pv = _dot(p, vext_sc[blk(vblk), :])
        acc_sc[qi] = jnp.concatenate([alpha, alpha], axis=1) * acc_sc[qi] + pv
        m_sc[qi] = m_new

    hh = t // 2

    def scores_diag(dst_s, dst_pm, qi):
        q0, q1 = split_q(qi)
        k_all = k_ref[qi * t:(qi + 1) * t, :]
        s_a = _dot_nt(jnp.concatenate([q0[:hh], q1[:hh]], axis=0), k_all[:hh])
        s_b = _dot_nt(jnp.concatenate([q0[hh:], q1[hh:]], axis=0), k_all)
        keep_a = (lax.broadcasted_iota(jnp.int32, (hh, hh), 1)
                  <= lax.broadcasted_iota(jnp.int32, (hh, hh), 0))
        keep_b = (lax.broadcasted_iota(jnp.int32, (hh, t), 1)
                  <= lax.broadcasted_iota(jnp.int32, (hh, t), 0) + hh)
        s_a = jnp.where(jnp.concatenate([keep_a, keep_a], axis=0), s_a, NEG_BIG)
        s_b = jnp.where(jnp.concatenate([keep_b, keep_b], axis=0), s_b, NEG_BIG)
        dst_s[0:t, 0:hh] = s_a
        dst_s[t:rows, :] = s_b
        dst_pm[0:t, :] = lane_tile_max(s_a)
        dst_pm[t:rows, :] = lane_tile_max(s_b)

    def process_diag(src_s, src_pm, qi):
        m_a = jnp.broadcast_to(jnp.max(src_pm[0:t, :], axis=-1, keepdims=True), (t, LANES))
        m_b = jnp.broadcast_to(jnp.max(src_pm[t:rows, :], axis=-1, keepdims=True), (t, LANES))
        pv_a = _dot(exp_tiles(src_s, 0, t, hh, m_a), vext_sc[qi * t:qi * t + hh, :])
        pv_b = _dot(exp_tiles(src_s, t, rows, t, m_b), vext_sc[qi * t:(qi + 1) * t, :])
        for dst, val_acc, val_m in ((0, pv_a[:hh], m_a[:hh]), (hh, pv_b[:hh], m_b[:hh]),
                                    (t, pv_a[hh:], m_a[hh:]), (t + hh, pv_b[hh:], m_b[hh:])):
            acc_sc[qi, dst:dst + hh, :] = val_acc
            m_sc[qi, dst:dst + hh, :] = val_m

    def finalize(qi):
        acc = acc_sc[qi]
        o = acc[:, :dv] / acc[:, dv:]
        od = o[:t] - lam * o[t:]
        y = _rms(od, sw_ref[...]) * (1.0 - lambda_init)
        o_ref[qi * t:(qi + 1) * t, :] = y.astype(BF16)

    bufs = ((sa_sc, pa_sc), (sb_sc, pb_sc))
    steps = []
    for qi in range(nq):
        steps.append((qi, qi, True))
        steps.extend((qi, j, False) for j in range(qi))
    def emit_scores(g):
        qi, kblk, diag = steps[g]
        if diag:
            scores_diag(*bufs[g % 2], qi)
        else:
            scores(*bufs[g % 2], qi, kblk)

    emit_scores(0)

    def emit(g0, g1, ml_chunk):
        ctx = None
        heads_done = 0
        every = max(1, (g1 - g0) // (ML_HEADS + 1))
        for g in range(g0, g1):
            if ml_chunk is not None and (g - g0) % every == 0:
                if ctx is None:
                    ctx = ml_gate(ml_chunk * ML_CHUNK)
                elif heads_done < ML_HEADS:
                    ml_head(ctx, heads_done)
                    heads_done += 1
            qi, kblk, diag = steps[g]
            if g + 1 < len(steps):
                emit_scores(g + 1)
            if diag:
                process_diag(*bufs[g % 2], qi)
            else:
                process(*bufs[g % 2], qi, kblk)
            if g + 1 == len(steps) or steps[g + 1][0] != qi:
                finalize(qi)
        if ml_chunk is not None:
            for h in range(heads_done, ML_HEADS):
                ml_head(ctx, h)

    one = jnp.minimum(pl.program_id(0), 0) + 1
    starts = list(range(0, len(steps), ATT_STEPS_PER_REGION))
    ml_chunks = t // ML_CHUNK
    assert len(starts) % ml_chunks == 0
    for r, g0 in enumerate(starts):
        g1 = min(g0 + ATT_STEPS_PER_REGION, len(steps))
        stride = len(starts) // ml_chunks
        ml_chunk = r // stride if r % stride == 0 else None

        def region(_, c, g0=g0, g1=g1, ml_chunk=ml_chunk):
            emit(g0, g1, ml_chunk)
            return c

        lax.fori_loop(0, one, region, 0)


def _mixers(proj, gates, lam_p, subln_w, gate_b, ml_norm_w, batch, seq, lambda_init):
    n = proj.shape[0]
    t = ATT_T
    nq = seq // t
    assert nq == DA_HEADS and t % ML_CHUNK == 0
    kern = functools.partial(_mixers_kernel, seq=seq, lambda_init=lambda_init)
    hb = PROJ_TN // LANES
    mw = ML_HEADS * ML_V_DIM
    const = lambda b, h: (0, 0)
    return pl.pallas_call(
        kern,
        grid=(batch, DA_HEADS),
        in_specs=[
            pl.BlockSpec((4, DA_HEAD_DIM), const),
            pl.BlockSpec((1, DA_V_DIM), const),
            pl.BlockSpec((seq, LANES), lambda b, h: (b, SEG_Q * hb + h)),
            pl.BlockSpec((seq, LANES), lambda b, h: (b, SEG_K * hb + h)),
            pl.BlockSpec((seq, LANES), lambda b, h: (b, SEG_V * hb + h)),
            pl.BlockSpec((t, PROJ_TN), lambda b, h: (b * nq + h, SEG_MQK)),
            pl.BlockSpec((t, PROJ_TN), lambda b, h: (b * nq + h, SEG_MV)),
            pl.BlockSpec((t, PROJ_TN), lambda b, h: (b * nq + h, SEG_MO)),
            pl.BlockSpec((t, LANES), lambda b, h: (b * nq + h, 0)),
            pl.BlockSpec((1, LANES), const),
            pl.BlockSpec((1, mw), const),
        ],
        out_specs=[
            pl.BlockSpec((seq, LANES), lambda b, h: (b, h)),
            pl.BlockSpec((t, mw), lambda b, h: (b * nq + h, 0)),
        ],
        out_shape=[
            jax.ShapeDtypeStruct((n, DA_HEADS * DA_V_DIM), BF16),
            jax.ShapeDtypeStruct((n, mw), BF16),
        ],
        scratch_shapes=[
            pltpu.VMEM((seq, 2 * LANES), BF16),
            pltpu.VMEM((2 * t, t), F32),
            pltpu.VMEM((2 * t, t), F32),
            pltpu.VMEM((2 * t, LANES), F32),
            pltpu.VMEM((2 * t, LANES), F32),
            pltpu.VMEM((nq, 2 * t, 2 * LANES), F32),
            pltpu.VMEM((nq, 2 * t, LANES), F32),
            pltpu.VMEM((ML_HEADS, ML_QK_DIM, 2 * LANES), F32),
            pltpu.VMEM((SUBLANES, LANES), F32),
            pltpu.VMEM((ML_CHUNK, ML_CHUNK), F32),
        ],
        compiler_params=pltpu.CompilerParams(
            dimension_semantics=("arbitrary", "arbitrary"),
            vmem_limit_bytes=VMEM_LIMIT_BYTES),
        name="mixers",
    )(lam_p, subln_w, proj, proj, proj, proj, proj, proj, gates, gate_b, ml_norm_w)


def _mlstm_parts(qk_ref, v_ref, so_ref, g_ref, gb_ref, nw_ref, out_ref, cext_sc, m_sc, mask_sc):
    L = ML_CHUNK
    dk, dv = ML_QK_DIM, ML_V_DIM
    qkw = ML_HEADS * dk
    nlt = L // LANES

    def init():
        cext_sc[...] = jnp.zeros(cext_sc.shape, F32)
        m_sc[...] = jnp.zeros(m_sc.shape, F32)
        row = lax.broadcasted_iota(jnp.int32, (L, L), 0)
        col = lax.broadcasted_iota(jnp.int32, (L, L), 1)
        mask_sc[...] = jnp.where(col <= row, 0.0, NEG_BIG)

    def gate(r0):
        rs = slice(r0, r0 + L)
        tri_b = (mask_sc[...] == 0.0).astype(BF16)
        rowi = lax.broadcasted_iota(jnp.int32, (L, LANES), 0)
        g = g_ref[rs, :] + gb_ref[...]
        logf = jnp.minimum(g, 0.0) - jnp.log(1.0 + jnp.exp(-jnp.abs(g)))
        hi = logf.astype(BF16)
        r1 = logf - hi.astype(F32)
        mid = r1.astype(BF16)
        lo = (r1 - mid.astype(F32)).astype(BF16)
        bcs = _dot(tri_b, hi) + _dot(tri_b, mid) + _dot(tri_b, lo)
        b_al = pltpu.roll(bcs, LANES - ML_HEADS, 1)
        a = g - b_al
        m_prev = m_sc[0:1, :]
        cm = a
        d = 1
        while d < L:
            cm = jnp.maximum(cm, jnp.where(rowi >= d, pltpu.roll(cm, d, 0), NEG_BIG))
            d *= 2
        u = jnp.maximum(cm, m_prev)
        b_last = b_al[L - 1:L, :]
        w_log = b_last + a
        m_new = jnp.maximum(b_last + m_prev, jnp.max(w_log, axis=0, keepdims=True))
        m_sc[0:1, :] = m_new
        return dict(rs=rs, u=u, mt=b_al + u, m_prev=m_prev,
                    decay=jnp.exp(b_last + m_prev - m_new),
                    a_t=a.T,
                    ws_t=jnp.exp(w_log - m_new).T)

    def head(ctx, h):
        rs = ctx["rs"]
        u_b = jnp.broadcast_to(ctx["u"][:, h:h + 1], (L, LANES))
        mt_b = jnp.broadcast_to(ctx["mt"][:, h:h + 1], (L, LANES))
        inter_b = jnp.exp(ctx["m_prev"][:, h:h + 1] - u_b)
        floor_b = jnp.exp(-mt_b)
        dexp = jnp.concatenate(
            [jnp.exp(ctx["a_t"][h:h + 1, t * LANES:(t + 1) * LANES] - u_b
                     + mask_sc[:, t * LANES:(t + 1) * LANES]) for t in range(nlt)], axis=1)
        qh = qk_ref[rs, h * dk:(h + 1) * dk]
        kh_t = qk_ref[rs, qkw + h * dk:qkw + (h + 1) * dk].T
        cext = cext_sc[h]
        vext = jnp.concatenate([v_ref[rs, h * dv:(h + 1) * dv], jnp.ones((L, LANES), BF16)],
                               axis=1)
        s = (_dot(qh, kh_t) * dexp).astype(BF16)
        hext = (_dot(s, vext)
                + jnp.concatenate([inter_b, inter_b], axis=1) * _dot(qh, cext.astype(BF16)))
        hm = hext[:, :dv] / jnp.maximum(jnp.abs(hext[:, dv:]), floor_b)
        y = _rms(hm, nw_ref[:, h * dv:(h + 1) * dv])
        y = so_ref[rs, h * dv:(h + 1) * dv].astype(F32) * y
        out_ref[rs, h * dv:(h + 1) * dv] = y.astype(BF16)
        kw_t = (kh_t.astype(F32) * ctx["ws_t"][h:h + 1, :]).astype(BF16)
        cext_sc[h] = ctx["decay"][:, h:h + 1] * cext + _dot(kw_t, vext)

    return init, gate, head


def _post_kernel(x_ref, oa_ref, ob_ref, g0_ref, g1_ref, wa_ref, wb_ref, wo_ref, nf_ref,
                 w1_ref, w2_ref, fw_ref, out_ref, *, final_norm):
    ya = _dot(oa_ref[...], wa_ref[...])
    yb = _dot(ob_ref[...], wb_ref[...])
    merged = g0_ref[...].astype(F32) * ya + g1_ref[...].astype(F32) * yb
    h = x_ref[...] + _dot(merged.astype(BF16), wo_ref[...])
    hn = _rms(h, nf_ref[...]).astype(BF16)
    acc = h
    for c0 in range(0, D_FF, FF_CHUNK):
        u = jnp.maximum(_dot(hn, w1_ref[:, c0:c0 + FF_CHUNK]), 0.0)
        acc = acc + _dot((u * u).astype(BF16), w2_ref[c0:c0 + FF_CHUNK, :])
    if final_norm:
        acc = _rms(acc, fw_ref[...])
    out_ref[...] = acc


def _post(x2, oa, ob, proj, wa, wb, wo, nf, w1, w2, fw, final_norm):
    n = x2.shape[0]
    tm = POST_TM
    const = lambda i: (0, 0)
    single = pl.Buffered(1)
    kern = functools.partial(_post_kernel, final_norm=final_norm)
    return pl.pallas_call(
        kern,
        grid=(n // tm,),
        in_specs=[
            pl.BlockSpec((tm, D_MODEL), lambda i: (i, 0)),
            pl.BlockSpec((tm, D_MODEL), lambda i: (i, 0)),
            pl.BlockSpec((tm, D_MODEL), lambda i: (i, 0)),
            pl.BlockSpec((tm, PROJ_TN), lambda i: (i, SEG_G0)),
            pl.BlockSpec((tm, PROJ_TN), lambda i: (i, SEG_G1)),
            pl.BlockSpec((D_MODEL, D_MODEL), const, pipeline_mode=single),
            pl.BlockSpec((D_MODEL, D_MODEL), const, pipeline_mode=single),
            pl.BlockSpec((D_MODEL, D_MODEL), const, pipeline_mode=single),
            pl.BlockSpec((1, D_MODEL), const),
            pl.BlockSpec((D_MODEL, D_FF), const, pipeline_mode=single),
            pl.BlockSpec((D_FF, D_MODEL), const, pipeline_mode=single),
            pl.BlockSpec((1, D_MODEL), const),
        ],
        out_specs=pl.BlockSpec((tm, D_MODEL), lambda i: (i, 0)),
        out_shape=jax.ShapeDtypeStruct((n, D_MODEL), F32),
        compiler_params=pltpu.CompilerParams(
            dimension_semantics=("arbitrary",),
            vmem_limit_bytes=VMEM_LIMIT_BYTES),
        name="post_mixer",
    )(x2, oa, ob, proj, proj, wa, wb, wo, nf, w1, w2, fw)


def kernel(x, positions, norm_mix_w, w_in, ml_gate_b, conv_w, conv_b, da_lambda, da_subln_w,
           ml_norm_w, w_proj_a, w_proj_b, w_out, norm_ffn_w, w_ff1, w_ff2, final_norm_w):
    batch, seq, _ = x.shape
    n = batch * seq
    depth = w_in.shape[0]
    assert seq % ATT_T == 0 and seq % PROJ_TM == 0 and n % POST_TM == 0

    da_w = DA_HEADS * 2 * DA_HEAD_DIM
    ml_qk = ML_HEADS * ML_QK_DIM
    ml_v = ML_HEADS * ML_V_DIM
    o_mq = 3 * da_w
    o_mv = o_mq + 2 * ml_qk
    o_gi = o_mv + ml_v
    o_mo = o_gi + 2 * ML_HEADS
    o_gate = o_mo + ml_v

    pos2 = positions.reshape(n, 1)
    inv = ROPE_THETA ** (-jnp.arange(0, ROPE_DIM, 2, dtype=F32) / ROPE_DIM)
    invf = jnp.tile(inv, LANES // (ROPE_DIM // 2)).reshape(1, LANES)

    h = x.reshape(n, D_MODEL)
    for l in range(depth):
        lambda_init = 0.8 - 0.6 * math.exp(-0.3 * l)
        w = w_in[l].astype(BF16)
        wa = w
        wb = w[:, o_mo:]
        wg = jnp.pad(w[:, o_gi:o_mo], ((0, 0), (0, LANES - 2 * ML_HEADS)))
        gate_b = jnp.pad(ml_gate_b[l], (0, LANES - 2 * ML_HEADS)).reshape(1, LANES)

        proj, gates = _inproj(h, pos2, norm_mix_w[l].reshape(1, D_MODEL), invf, conv_w[l],
                              conv_b[l].reshape(1, -1), wa, wb, wg, seq)
        oa, ob = _mixers(proj, gates, da_lambda[l], da_subln_w[l].reshape(1, DA_V_DIM), gate_b,
                         ml_norm_w[l].reshape(1, -1), batch, seq, lambda_init)
        h = _post(h, oa, ob, proj,
                  w_proj_a[l].astype(BF16), w_proj_b[l].astype(BF16), w_out[l].astype(BF16),
                  norm_ffn_w[l].reshape(1, D_MODEL), w_ff1[l].astype(BF16),
                  w_ff2[l].astype(BF16), final_norm_w.reshape(1, D_MODEL),
                  final_norm=(l == depth - 1))
    return h.reshape(batch, seq, D_MODEL)
```

```python
import functools
import math

import jax
import jax.numpy as jnp
from jax import lax
from jax.experimental import pallas as pl
from jax.experimental.pallas import tpu as pltpu

F32 = jnp.float32
BF16 = jnp.bfloat16

D_MODEL = 1024
DA_HEADS = 8
DA_HEAD_DIM = 64
DA_V_DIM = 128
ROPE_DIM = 16
ROPE_THETA = 500000.0
ML_HEADS = 8
ML_QK_DIM = 64
ML_V_DIM = 128
CONV_WIDTH = 4
D_FF = 4 * D_MODEL
EPS = 1e-6

LANES = 128
SUBLANES = 8
VMEM_LIMIT_BYTES = 56 * 1024 * 1024

PROJ_TM = 512
PROJ_TN = 1024
PROJ_CN = 256
ATT_T = 512
ATT_STEPS_PER_REGION = 9
ML_CHUNK = 256
POST_TM = 512
FF_CHUNK = 1024

SEG_Q, SEG_K, SEG_V, SEG_MQK, SEG_MV, SEG_MO, SEG_G0, SEG_G1 = range(8)
N_SEG = 8

NEG_BIG = -1e30
LOG2E = 1.4426950408889634


def _dot(a, b):
    return jnp.dot(a, b, preferred_element_type=F32)


def _dot_nt(a, b):
    return lax.dot_general(a, b, (((1,), (1,)), ((), ())), preferred_element_type=F32)


def _dot_tn(a, b):
    return lax.dot_general(a, b, (((0,), (0,)), ((), ())), preferred_element_type=F32)


def _sigmoid(x):
    return 0.5 * jnp.tanh(0.5 * x) + 0.5


def _rms(x, w):
    return x * lax.rsqrt(jnp.mean(x * x, axis=-1, keepdims=True) + EPS) * w


def _inproj_kernel(x_ref, posd_ref, nw_ref, invf_ref, cw_ref, cb_ref, wa_ref, wb_ref, wg_ref,
                   out_ref, gates_ref, cbuf, xn_sc, *, tiles_per_seq):
    tm = PROJ_TM
    cn = PROJ_CN
    i = pl.program_id(0)

    hist_rows = CONV_WIDTH - 1
    first = i % tiles_per_seq == 0

    @pl.when(first)
    def _():
        cbuf[0:SUBLANES, :] = jnp.zeros((SUBLANES, PROJ_TN), F32)

    @pl.when(jnp.logical_not(first))
    def _():
        cbuf[0:SUBLANES, :] = cbuf[tm:tm + SUBLANES, :]

    xn_sc[...] = _rms(x_ref[...], nw_ref[...]).astype(BF16)
    gates_ref[...] = _dot(xn_sc[...], wg_ref[...])
    half = ROPE_DIM // 2

    def chunk(seg, c_in_seg, tables):
        c0 = seg * PROJ_TN + c_in_seg
        if seg < SEG_MO:
            w_chunk = wa_ref[:, c0:c0 + cn]
        else:
            w_chunk = wb_ref[:, c0 - SEG_MO * PROJ_TN:c0 - SEG_MO * PROJ_TN + cn]
        acc = _dot(xn_sc[...], w_chunk)
        if seg in (SEG_Q, SEG_K):
            cos, sin_signed, partner_lane = tables
            scale = DA_HEAD_DIM ** -0.5 * LOG2E if seg == SEG_Q else 1.0
            for l0 in range(0, cn, LANES):
                xc = acc[:, l0:l0 + LANES]
                r = xc * cos + jnp.take_along_axis(xc, partner_lane, axis=1) * sin_signed
                out_ref[:, c0 + l0:c0 + l0 + LANES] = (r * scale).astype(BF16)
        elif seg == SEG_MQK:
            m0 = c0 - SEG_MQK * PROJ_TN
            cbuf[SUBLANES:SUBLANES + tm, m0:m0 + cn] = acc
            conv = cb_ref[:, m0:m0 + cn]
            for j in range(CONV_WIDTH):
                off = SUBLANES - hist_rows + j
                conv = conv + (cw_ref[j:j + 1, m0:m0 + cn]
                               * cbuf[off:off + tm, m0:m0 + cn])
            y = conv * _sigmoid(conv)
            if m0 < ML_HEADS * ML_QK_DIM:
                y = y * (ML_QK_DIM ** -0.5)
            out_ref[:, c0:c0 + cn] = y.astype(BF16)
        elif seg in (SEG_V, SEG_MV):
            out_ref[:, c0:c0 + cn] = acc.astype(BF16)
        else:
            out_ref[:, c0:c0 + cn] = _sigmoid(acc).astype(BF16)

    for c_in_seg in range(0, PROJ_TN, cn):
        for seg in (SEG_V, SEG_MQK, SEG_MV, SEG_G0, SEG_MO, SEG_G1):
            chunk(seg, c_in_seg, None)

    rpd = LANES // half
    ang = posd_ref[...].astype(F32) * invf_ref[...]
    lane_id = lax.broadcasted_iota(jnp.int32, (tm, LANES), 1)
    row_id = lax.broadcasted_iota(jnp.int32, (tm, LANES), 0)
    own = (lane_id // half) == (row_id % rpd)
    expand = ((lax.broadcasted_iota(jnp.int32, (LANES, LANES), 0) % half)
              == (lax.broadcasted_iota(jnp.int32, (LANES, LANES), 1) % half)).astype(BF16)

    def expand_rows(d):
        rep = jnp.concatenate([jnp.broadcast_to(d[r:r + 1, :], (rpd, LANES))
                               for r in range(tm // rpd)], axis=0)
        a = jnp.where(own, rep, 0.0)
        hi = a.astype(BF16)
        lo = (a - hi.astype(F32)).astype(BF16)
        return _dot(hi, expand) + _dot(lo, expand)

    c = expand_rows(jnp.cos(ang))
    s = expand_rows(jnp.sin(ang))
    sub = lane_id % DA_HEAD_DIM
    partner_lane = jnp.where(sub < half, lane_id + half,
                             jnp.where(sub < ROPE_DIM, lane_id - half, lane_id))
    tables = (jnp.where(sub < ROPE_DIM, c, 1.0),
              jnp.where(sub < half, -s, jnp.where(sub < ROPE_DIM, s, 0.0)), partner_lane)
    for c_in_seg in range(0, PROJ_TN, cn):
        for seg in (SEG_Q, SEG_K):
            chunk(seg, c_in_seg, tables)


def _inproj(x2, pos2, nw, invf, conv_w, conv_b, wa, wb, wg, seq):
    n = x2.shape[0]
    tm = PROJ_TM
    const = lambda i: (0, 0)
    single = pl.Buffered(1)
    kern = functools.partial(_inproj_kernel, tiles_per_seq=seq // tm)
    return pl.pallas_call(
        kern,
        grid=(n // tm,),
        in_specs=[
            pl.BlockSpec((tm, D_MODEL), lambda i: (i, 0)),
            pl.BlockSpec((tm // (LANES // (ROPE_DIM // 2)), LANES), lambda i: (i, 0)),
            pl.BlockSpec((1, D_MODEL), const),
            pl.BlockSpec((1, LANES), const),
            pl.BlockSpec((CONV_WIDTH, PROJ_TN), const),
            pl.BlockSpec((1, PROJ_TN), const),
            pl.BlockSpec((D_MODEL, SEG_MO * PROJ_TN), const, pipeline_mode=single),
            pl.BlockSpec((D_MODEL, (N_SEG - SEG_MO) * PROJ_TN), const, pipeline_mode=single),
            pl.BlockSpec((D_MODEL, LANES), const, pipeline_mode=single),
        ],
        out_specs=[
            pl.BlockSpec((tm, N_SEG * PROJ_TN), lambda i: (i, 0)),
            pl.BlockSpec((tm, LANES), lambda i: (i, 0)),
        ],
        out_shape=[
            jax.ShapeDtypeStruct((n, N_SEG * PROJ_TN), BF16),
            jax.ShapeDtypeStruct((n, LANES), F32),
        ],
        scratch_shapes=[
            pltpu.VMEM((tm + SUBLANES, PROJ_TN), F32),
            pltpu.VMEM((tm, D_MODEL), BF16),
        ],
        compiler_params=pltpu.CompilerParams(
            dimension_semantics=("arbitrary",),
            vmem_limit_bytes=VMEM_LIMIT_BYTES),
        name="inproj",
    )(x2, pos2, nw, invf, conv_w, conv_b, wa, wb, wg)


def _mixers_kernel(lam_ref, sw_ref, q_ref, k_ref, v_ref, mqk_ref, mv_ref, mso_ref, mg_ref,
                   mgb_ref, mnw_ref, o_ref, ob_ref, vext_sc, sa_sc, sb_sc, pa_sc, pb_sc, acc_sc,
                   m_sc, cext_sc, mm_sc, mask_sc, *, seq, lambda_init):
    ml_init, ml_gate, ml_head = _mlstm_parts(mqk_ref, mv_ref, mso_ref, mg_ref, mgb_ref, mnw_ref,
                                             ob_ref, cext_sc, mm_sc, mask_sc)

    @pl.when(pl.program_id(1) == 0)
    def _():
        ml_init()

    t = ATT_T
    hd = DA_HEAD_DIM
    dv = DA_V_DIM
    nq = seq // t
    rows = 2 * t

    @pl.when((pl.program_id(0) == 0) & (pl.program_id(1) == 0))
    def _():
        vext_sc[:, dv:] = jnp.ones((seq, LANES), BF16)

    vext_sc[:, :dv] = v_ref[...]

    lp = lam_ref[...]
    lam = (jnp.exp(jnp.sum(lp[0:1] * lp[1:2], axis=-1, keepdims=True))
           - jnp.exp(jnp.sum(lp[2:3] * lp[3:4], axis=-1, keepdims=True)) + lambda_init)

    def blk(i):
        return pl.ds(i * t if isinstance(i, int) else pl.multiple_of(i * t, t), t)

    def split_q(qi):
        q = q_ref[blk(qi), :]
        lane = lax.broadcasted_iota(jnp.int32, (t, LANES), 1)
        zero = jnp.zeros_like(q)
        return jnp.where(lane < hd, q, zero), jnp.where(lane >= hd, q, zero)

    def lane_tile_max(s):
        pm = s[:, 0:LANES]
        for c in range(1, s.shape[1] // LANES):
            pm = jnp.maximum(pm, s[:, c * LANES:(c + 1) * LANES])
        return pm

    def exp_tiles(src, r0, r1, width, m):
        return jnp.concatenate(
            [jnp.exp2(src[r0:r1, c * LANES:(c + 1) * LANES] - m).astype(BF16)
             for c in range(width // LANES)], axis=1)

    def scores(dst_s, dst_pm, qi, kblk):
        q0, q1 = split_q(qi)
        s = _dot_nt(jnp.concatenate([q0, q1], axis=0), k_ref[blk(kblk), :])
        dst_s[...] = s
        dst_pm[...] = lane_tile_max(s)

    def process(src_s, src_pm, qi, vblk):
        m_prev = m_sc[qi]
        m_new = jnp.maximum(m_prev, jnp.max(src_pm[...], axis=-1, keepdims=True))
        alpha = jnp.exp2(m_prev - m_new)
        p = exp_tiles(src_s, 0, rows, t, m_new)
        pv = _dot(p, vext_sc[blk(vblk), :])
        acc_sc[qi] = jnp.concatenate([alpha, alpha], axis=1) * acc_sc[qi] + pv
        m_sc[qi] = m_new

    hh = t // 2

    def scores_diag(dst_s, dst_pm, qi):
        q0, q1 = split_q(qi)
        k_all = k_ref[qi * t:(qi + 1) * t, :]
        s_a = _dot_nt(jnp.concatenate([q0[:hh], q1[:hh]], axis=0), k_all[:hh])
        s_b = _dot_nt(jnp.concatenate([q0[hh:], q1[hh:]], axis=0), k_all)
        keep_a = (lax.broadcasted_iota(jnp.int32, (hh, hh), 1)
                  <= lax.broadcasted_iota(jnp.int32, (hh, hh), 0))
        keep_b = (lax.broadcasted_iota(jnp.int32, (hh, t), 1)
                  <= lax.broadcasted_iota(jnp.int32, (hh, t), 0) + hh)
        s_a = jnp.where(jnp.concatenate([keep_a, keep_a], axis=0), s_a, NEG_BIG)
        s_b = jnp.where(jnp.concatenate([keep_b, keep_b], axis=0), s_b, NEG_BIG)
        dst_s[0:t, 0:hh] = s_a
        dst_s[t:rows, :] = s_b
        dst_pm[0:t, :] = lane_tile_max(s_a)
        dst_pm[t:rows, :] = lane_tile_max(s_b)

    def process_diag(src_s, src_pm, qi):
        m_a = jnp.broadcast_to(jnp.max(src_pm[0:t, :], axis=-1, keepdims=True), (t, LANES))
        m_b = jnp.broadcast_to(jnp.max(src_pm[t:rows, :], axis=-1, keepdims=True), (t, LANES))
        pv_a = _dot(exp_tiles(src_s, 0, t, hh, m_a), vext_sc[qi * t:qi * t + hh, :])
        pv_b = _dot(exp_tiles(src_s, t, rows, t, m_b), vext_sc[qi * t:(qi + 1) * t, :])
        for dst, val_acc, val_m in ((0, pv_a[:hh], m_a[:hh]), (hh, pv_b[:hh], m_b[:hh]),
                                    (t, pv_a[hh:], m_a[hh:]), (t + hh, pv_b[hh:], m_b[hh:])):
            acc_sc[qi, dst:dst + hh, :] = val_acc
            m_sc[qi, dst:dst + hh, :] = val_m

    def finalize(qi):
        acc = acc_sc[qi]
        o = acc[:, :dv] / acc[:, dv:]
        od = o[:t] - lam * o[t:]
        y = _rms(od, sw_ref[...]) * (1.0 - lambda_init)
        o_ref[qi * t:(qi + 1) * t, :] = y.astype(BF16)

    bufs = ((sa_sc, pa_sc), (sb_sc, pb_sc))
    steps = []
    for qi in range(nq):
        steps.append((qi, qi, True))
        steps.extend((qi, j, False) for j in range(qi))
    def emit_scores(g):
        qi, kblk, diag = steps[g]
        if diag:
            scores_diag(*bufs[g % 2], qi)
        else:
            scores(*bufs[g % 2], qi, kblk)

    emit_scores(0)

    def emit(g0, g1, ml_chunk):
        ctx = None
        heads_done = 0
        every = max(1, (g1 - g0) // (ML_HEADS + 1))
        for g in range(g0, g1):
            if ml_chunk is not None and (g - g0) % every == 0:
                if ctx is None:
                    ctx = ml_gate(ml_chunk * ML_CHUNK)
                elif heads_done < ML_HEADS:
                    ml_head(ctx, heads_done)
                    heads_done += 1
            qi, kblk, diag = steps[g]
            if g + 1 < len(steps):
                emit_scores(g + 1)
            if diag:
                process_diag(*bufs[g % 2], qi)
            else:
                process(*bufs[g % 2], qi, kblk)
            if g + 1 == len(steps) or steps[g + 1][0] != qi:
                finalize(qi)
        if ml_chunk is not None:
            for h in range(heads_done, ML_HEADS):
                ml_head(ctx, h)

    one = jnp.minimum(pl.program_id(0), 0) + 1
    starts = list(range(0, len(steps), ATT_STEPS_PER_REGION))
    ml_chunks = t // ML_CHUNK
    assert len(starts) % ml_chunks == 0
    for r, g0 in enumerate(starts):
        g1 = min(g0 + ATT_STEPS_PER_REGION, len(steps))
        stride = len(starts) // ml_chunks
        ml_chunk = r // stride if r % stride == 0 else None

        def region(_, c, g0=g0, g1=g1, ml_chunk=ml_chunk):
            emit(g0, g1, ml_chunk)
            return c

        lax.fori_loop(0, one, region, 0)


def _mixers(proj, gates, lam_p, subln_w, gate_b, ml_norm_w, batch, seq, lambda_init):
    n = proj.shape[0]
    t = ATT_T
    nq = seq // t
    assert nq == DA_HEADS and t % ML_CHUNK == 0
    kern = functools.partial(_mixers_kernel, seq=seq, lambda_init=lambda_init)
    hb = PROJ_TN // LANES
    mw = ML_HEADS * ML_V_DIM
    const = lambda b, h: (0, 0)
    return pl.pallas_call(
        kern,
        grid=(batch, DA_HEADS),
        in_specs=[
            pl.BlockSpec((4, DA_HEAD_DIM), const),
            pl.BlockSpec((1, DA_V_DIM), const),
            pl.BlockSpec((seq, LANES), lambda b, h: (b, SEG_Q * hb + h)),
            pl.BlockSpec((seq, LANES), lambda b, h: (b, SEG_K * hb + h)),
            pl.BlockSpec((seq, LANES), lambda b, h: (b, SEG_V * hb + h)),
            pl.BlockSpec((t, PROJ_TN), lambda b, h: (b * nq + h, SEG_MQK)),
            pl.BlockSpec((t, PROJ_TN), lambda b, h: (b * nq + h, SEG_MV)),
            pl.BlockSpec((t, PROJ_TN), lambda b, h: (b * nq + h, SEG_MO)),
            pl.BlockSpec((t, LANES), lambda b, h: (b * nq + h, 0)),
            pl.BlockSpec((1, LANES), const),
            pl.BlockSpec((1, mw), const),
        ],
        out_specs=[
            pl.BlockSpec((seq, LANES), lambda b, h: (b, h)),
            pl.BlockSpec((t, mw), lambda b, h: (b * nq + h, 0)),
        ],
        out_shape=[
            jax.ShapeDtypeStruct((n, DA_HEADS * DA_V_DIM), BF16),
            jax.ShapeDtypeStruct((n, mw), BF16),
        ],
        scratch_shapes=[
            pltpu.VMEM((seq, 2 * LANES), BF16),
            pltpu.VMEM((2 * t, t), F32),
            pltpu.VMEM((2 * t, t), F32),
            pltpu.VMEM((2 * t, LANES), F32),
            pltpu.VMEM((2 * t, LANES), F32),
            pltpu.VMEM((nq, 2 * t, 2 * LANES), F32),
            pltpu.VMEM((nq, 2 * t, LANES), F32),
            pltpu.VMEM((ML_HEADS, ML_QK_DIM, 2 * LANES), F32),
            pltpu.VMEM((SUBLANES, LANES), F32),
            pltpu.VMEM((ML_CHUNK, ML_CHUNK), F32),
        ],
        compiler_params=pltpu.CompilerParams(
            dimension_semantics=("arbitrary", "arbitrary"),
            vmem_limit_bytes=VMEM_LIMIT_BYTES),
        name="mixers",
    )(lam_p, subln_w, proj, proj, proj, proj, proj, proj, gates, gate_b, ml_norm_w)


def _mlstm_parts(qk_ref, v_ref, so_ref, g_ref, gb_ref, nw_ref, out_ref, cext_sc, m_sc, mask_sc):
    L = ML_CHUNK
    dk, dv = ML_QK_DIM, ML_V_DIM
    qkw = ML_HEADS * dk
    nlt = L // LANES

    def init():
        cext_sc[...] = jnp.zeros(cext_sc.shape, F32)
        m_sc[...] = jnp.zeros(m_sc.shape, F32)
        row = lax.broadcasted_iota(jnp.int32, (L, L), 0)
        col = lax.broadcasted_iota(jnp.int32, (L, L), 1)
        mask_sc[...] = jnp.where(col <= row, 0.0, NEG_BIG)

    def gate(r0):
        rs = slice(r0, r0 + L)
        tri_b = (mask_sc[...] == 0.0).astype(BF16)
        rowi = lax.broadcasted_iota(jnp.int32, (L, LANES), 0)
        g = g_ref[rs, :] + gb_ref[...]
        logf = jnp.minimum(g, 0.0) - jnp.log(1.0 + jnp.exp(-jnp.abs(g)))
        hi = logf.astype(BF16)
        r1 = logf - hi.astype(F32)
        mid = r1.astype(BF16)
        lo = (r1 - mid.astype(F32)).astype(BF16)
        bcs = _dot(tri_b, hi) + _dot(tri_b, mid) + _dot(tri_b, lo)
        b_al = pltpu.roll(bcs, LANES - ML_HEADS, 1)
        a = g - b_al
        m_prev = m_sc[0:1, :]
        cm = a
        d = 1
        while d < L:
            cm = jnp.maximum(cm, jnp.where(rowi >= d, pltpu.roll(cm, d, 0), NEG_BIG))
            d *= 2
        u = jnp.maximum(cm, m_prev)
        b_last = b_al[L - 1:L, :]
        w_log = b_last + a
        m_new = jnp.maximum(b_last + m_prev, jnp.max(w_log, axis=0, keepdims=True))
        m_sc[0:1, :] = m_new
        return dict(rs=rs, u=u, mt=b_al + u, m_prev=m_prev,
                    decay=jnp.exp(b_last + m_prev - m_new),
                    a_t=a.T,
                    ws_t=jnp.exp(w_log - m_new).T)

    def head(ctx, h):
        rs = ctx["rs"]
        u_b = jnp.broadcast_to(ctx["u"][:, h:h + 1], (L, LANES))
        mt_b = jnp.broadcast_to(ctx["mt"][:, h:h + 1], (L, LANES))
        inter_b = jnp.exp(ctx["m_prev"][:, h:h + 1] - u_b)
        floor_b = jnp.exp(-mt_b)
        dexp = jnp.concatenate(
            [jnp.exp(ctx["a_t"][h:h + 1, t * LANES:(t + 1) * LANES] - u_b
                     + mask_sc[:, t * LANES:(t + 1) * LANES]) for t in range(nlt)], axis=1)
        qh = qk_ref[rs, h * dk:(h + 1) * dk]
        kh_t = qk_ref[rs, qkw + h * dk:qkw + (h + 1) * dk].T
        cext = cext_sc[h]
        vext = jnp.concatenate([v_ref[rs, h * dv:(h + 1) * dv], jnp.ones((L, LANES), BF16)],
                               axis=1)
        s = (_dot(qh, kh_t) * dexp).astype(BF16)
        hext = (_dot(s, vext)
                + jnp.concatenate([inter_b, inter_b], axis=1) * _dot(qh, cext.astype(BF16)))
        hm = hext[:, :dv] / jnp.maximum(jnp.abs(hext[:, dv:]), floor_b)
        y = _rms(hm, nw_ref[:, h * dv:(h + 1) * dv])
        y = so_ref[rs, h * dv:(h + 1) * dv].astype(F32) * y
        out_ref[rs, h * dv:(h + 1) * dv] = y.astype(BF16)
        kw_t = (kh_t.astype(F32) * ctx["ws_t"][h:h + 1, :]).astype(BF16)
        cext_sc[h] = ctx["decay"][:, h:h + 1] * cext + _dot(kw_t, vext)

    return init, gate, head


def _post_kernel(x_ref, oa_ref, ob_ref, g0_ref, g1_ref, wa_ref, wb_ref, wo_ref, nf_ref,
                 w1_ref, w2_ref, fw_ref, out_ref, *, final_norm):
    ya = _dot(oa_ref[...], wa_ref[...])
    yb = _dot(ob_ref[...], wb_ref[...])
    merged = g0_ref[...].astype(F32) * ya + g1_ref[...].astype(F32) * yb
    h = x_ref[...] + _dot(merged.astype(BF16), wo_ref[...])
    hn = _rms(h, nf_ref[...]).astype(BF16)
    acc = h
    for c0 in range(0, D_FF, FF_CHUNK):
        u = jnp.maximum(_dot(hn, w1_ref[:, c0:c0 + FF_CHUNK]), 0.0)
        acc = acc + _dot((u * u).astype(BF16), w2_ref[c0:c0 + FF_CHUNK, :])
    if final_norm:
        acc = _rms(acc, fw_ref[...])
    out_ref[...] = acc


def _post(x2, oa, ob, proj, wa, wb, wo, nf, w1, w2, fw, final_norm):
    n = x2.shape[0]
    tm = POST_TM
    const = lambda i: (0, 0)
    single = pl.Buffered(1)
    kern = functools.partial(_post_kernel, final_norm=final_norm)
    return pl.pallas_call(
        kern,
        grid=(n // tm,),
        in_specs=[
            pl.BlockSpec((tm, D_MODEL), lambda i: (i, 0)),
            pl.BlockSpec((tm, D_MODEL), lambda i: (i, 0)),
            pl.BlockSpec((tm, D_MODEL), lambda i: (i, 0)),
            pl.BlockSpec((tm, PROJ_TN), lambda i: (i, SEG_G0)),
            pl.BlockSpec((tm, PROJ_TN), lambda i: (i, SEG_G1)),
            pl.BlockSpec((D_MODEL, D_MODEL), const, pipeline_mode=single),
            pl.BlockSpec((D_MODEL, D_MODEL), const, pipeline_mode=single),
            pl.BlockSpec((D_MODEL, D_MODEL), const, pipeline_mode=single),
            pl.BlockSpec((1, D_MODEL), const),
            pl.BlockSpec((D_MODEL, D_FF), const, pipeline_mode=single),
            pl.BlockSpec((D_FF, D_MODEL), const, pipeline_mode=single),
            pl.BlockSpec((1, D_MODEL), const),
        ],
        out_specs=pl.BlockSpec((tm, D_MODEL), lambda i: (i, 0)),
        out_shape=jax.ShapeDtypeStruct((n, D_MODEL), F32),
        compiler_params=pltpu.CompilerParams(
            dimension_semantics=("arbitrary",),
            vmem_limit_bytes=VMEM_LIMIT_BYTES),
        name="post_mixer",
    )(x2, oa, ob, proj, proj, wa, wb, wo, nf, w1, w2, fw)


def kernel(x, positions, norm_mix_w, w_in, ml_gate_b, conv_w, conv_b, da_lambda, da_subln_w,
           ml_norm_w, w_proj_a, w_proj_b, w_out, norm_ffn_w, w_ff1, w_ff2, final_norm_w):
    batch, seq, _ = x.shape
    n = batch * seq
    depth = w_in.shape[0]
    assert seq % ATT_T == 0 and seq % PROJ_TM == 0 and n % POST_TM == 0

    da_w = DA_HEADS * 2 * DA_HEAD_DIM
    ml_qk = ML_HEADS * ML_QK_DIM
    ml_v = ML_HEADS * ML_V_DIM
    o_mq = 3 * da_w
    o_mv = o_mq + 2 * ml_qk
    o_gi = o_mv + ml_v
    o_mo = o_gi + 2 * ML_HEADS
    o_gate = o_mo + ml_v

    pos2 = jnp.repeat(positions.reshape(n), ROPE_DIM // 2).reshape(-1, LANES)
    inv = ROPE_THETA ** (-jnp.arange(0, ROPE_DIM, 2, dtype=F32) / ROPE_DIM)
    invf = jnp.tile(inv, LANES // (ROPE_DIM // 2)).reshape(1, LANES)

    h = x.reshape(n, D_MODEL)
    for l in range(depth):
        lambda_init = 0.8 - 0.6 * math.exp(-0.3 * l)
        w = w_in[l].astype(BF16)
        wa = w
        wb = w[:, o_mo:]
        wg = jnp.pad(w[:, o_gi:o_mo], ((0, 0), (0, LANES - 2 * ML_HEADS)))
        gate_b = jnp.pad(ml_gate_b[l], (0, LANES - 2 * ML_HEADS)).reshape(1, LANES)

        proj, gates = _inproj(h, pos2, norm_mix_w[l].reshape(1, D_MODEL), invf, conv_w[l],
                              conv_b[l].reshape(1, -1), wa, wb, wg, seq)
        oa, ob = _mixers(proj, gates, da_lambda[l], da_subln_w[l].reshape(1, DA_V_DIM), gate_b,
                         ml_norm_w[l].reshape(1, -1), batch, seq, lambda_init)
        h = _post(h, oa, ob, proj,
                  w_proj_a[l].astype(BF16), w_proj_b[l].astype(BF16), w_out[l].astype(BF16),
                  norm_ffn_w[l].reshape(1, D_MODEL), w_ff1[l].astype(BF16),
                  w_ff2[l].astype(BF16), final_norm_w.reshape(1, D_MODEL),
                  final_norm=(l == depth - 1))
    return h.reshape(batch, seq, D_MODEL)
```

```python
import functools
import math

import jax
import jax.numpy as jnp
from jax import lax
from jax.experimental import pallas as pl
from jax.experimental.pallas import tpu as pltpu

F32 = jnp.float32
BF16 = jnp.bfloat16

D_MODEL = 1024
DA_HEADS = 8
DA_HEAD_DIM = 64
DA_V_DIM = 128
ROPE_DIM = 16
ROPE_THETA = 500000.0
ML_HEADS = 8
ML_QK_DIM = 64
ML_V_DIM = 128
CONV_WIDTH = 4
D_FF = 4 * D_MODEL
EPS = 1e-6

LANES = 128
SUBLANES = 8
VMEM_LIMIT_BYTES = 56 * 1024 * 1024

PROJ_TM = 512
PROJ_TN = 1024
PROJ_CN = 256
ATT_T = 512
ATT_STEPS_PER_REGION = 9
ML_CHUNK = 256
POST_TM = 512
FF_CHUNK = 1024

SEG_Q, SEG_K, SEG_V, SEG_MQK, SEG_MV, SEG_MO, SEG_G0, SEG_G1 = range(8)
N_SEG = 8

NEG_BIG = -1e30
LOG2E = 1.4426950408889634


def _dot(a, b):
    return jnp.dot(a, b, preferred_element_type=F32)


def _dot_nt(a, b):
    return lax.dot_general(a, b, (((1,), (1,)), ((), ())), preferred_element_type=F32)


def _dot_tn(a, b):
    return lax.dot_general(a, b, (((0,), (0,)), ((), ())), preferred_element_type=F32)


def _sigmoid(x):
    return 0.5 * jnp.tanh(0.5 * x) + 0.5


def _rms(x, w):
    return x * lax.rsqrt(jnp.mean(x * x, axis=-1, keepdims=True) + EPS) * w


def _inproj_kernel(x_ref, posd_ref, nw_ref, invf_ref, cw_ref, cb_ref, wa_ref, wb_ref, wg_ref,
                   out_ref, gates_ref, cbuf, xn_sc, *, tiles_per_seq):
    tm = PROJ_TM
    cn = PROJ_CN
    i = pl.program_id(0)

    hist_rows = CONV_WIDTH - 1
    first = i % tiles_per_seq == 0

    @pl.when(first)
    def _():
        cbuf[0:SUBLANES, :] = jnp.zeros((SUBLANES, PROJ_TN), F32)

    @pl.when(jnp.logical_not(first))
    def _():
        cbuf[0:SUBLANES, :] = cbuf[tm:tm + SUBLANES, :]

    xn_sc[...] = _rms(x_ref[...], nw_ref[...]).astype(BF16)
    gates_ref[...] = _dot(xn_sc[...], wg_ref[...])
    half = ROPE_DIM // 2

    def chunk(seg, c_in_seg, tables):
        c0 = seg * PROJ_TN + c_in_seg
        if seg < SEG_MO:
            w_chunk = wa_ref[:, c0:c0 + cn]
        else:
            w_chunk = wb_ref[:, c0 - SEG_MO * PROJ_TN:c0 - SEG_MO * PROJ_TN + cn]
        acc = _dot(xn_sc[...], w_chunk)
        if seg in (SEG_Q, SEG_K):
            cos, sin_signed, partner_lane = tables
            scale = DA_HEAD_DIM ** -0.5 * LOG2E if seg == SEG_Q else 1.0
            for l0 in range(0, cn, LANES):
                xc = acc[:, l0:l0 + LANES]
                r = xc * cos + jnp.take_along_axis(xc, partner_lane, axis=1) * sin_signed
                out_ref[:, c0 + l0:c0 + l0 + LANES] = (r * scale).astype(BF16)
        elif seg == SEG_MQK:
            m0 = c0 - SEG_MQK * PROJ_TN
            cbuf[SUBLANES:SUBLANES + tm, m0:m0 + cn] = acc
            conv = cb_ref[:, m0:m0 + cn]
            for j in range(CONV_WIDTH):
                off = SUBLANES - hist_rows + j
                conv = conv + (cw_ref[j:j + 1, m0:m0 + cn]
                               * cbuf[off:off + tm, m0:m0 + cn])
            y = conv * _sigmoid(conv)
            if m0 < ML_HEADS * ML_QK_DIM:
                y = y * (ML_QK_DIM ** -0.5)
            out_ref[:, c0:c0 + cn] = y.astype(BF16)
        elif seg in (SEG_V, SEG_MV):
            out_ref[:, c0:c0 + cn] = acc.astype(BF16)
        else:
            out_ref[:, c0:c0 + cn] = _sigmoid(acc).astype(BF16)

    rpd = LANES // half
    ang = posd_ref[...].astype(F32) * invf_ref[...]
    lane_id = lax.broadcasted_iota(jnp.int32, (tm, LANES), 1)
    row_id = lax.broadcasted_iota(jnp.int32, (tm, LANES), 0)
    own = (lane_id // half) == (row_id % rpd)
    expand = ((lax.broadcasted_iota(jnp.int32, (LANES, LANES), 0) % half)
              == (lax.broadcasted_iota(jnp.int32, (LANES, LANES), 1) % half)).astype(BF16)

    def expand_rows(d):
        rep = jnp.concatenate([jnp.broadcast_to(d[r:r + 1, :], (rpd, LANES))
                               for r in range(tm // rpd)], axis=0)
        a = jnp.where(own, rep, 0.0)
        hi = a.astype(BF16)
        lo = (a - hi.astype(F32)).astype(BF16)
        return _dot(hi, expand) + _dot(lo, expand)

    c = expand_rows(jnp.cos(ang))
    s = expand_rows(jnp.sin(ang))
    sub = lane_id % DA_HEAD_DIM
    partner_lane = jnp.where(sub < half, lane_id + half,
                             jnp.where(sub < ROPE_DIM, lane_id - half, lane_id))
    tables = (jnp.where(sub < ROPE_DIM, c, 1.0),
              jnp.where(sub < half, -s, jnp.where(sub < ROPE_DIM, s, 0.0)), partner_lane)
    for c_in_seg in range(0, PROJ_TN, cn):
        for seg in (SEG_V, SEG_Q, SEG_MV, SEG_MQK, SEG_G0, SEG_K, SEG_MO, SEG_G1):
            chunk(seg, c_in_seg, tables)


def _inproj(x2, pos2, nw, invf, conv_w, conv_b, wa, wb, wg, seq):
    n = x2.shape[0]
    tm = PROJ_TM
    const = lambda i: (0, 0)
    single = pl.Buffered(1)
    kern = functools.partial(_inproj_kernel, tiles_per_seq=seq // tm)
    return pl.pallas_call(
        kern,
        grid=(n // tm,),
        in_specs=[
            pl.BlockSpec((tm, D_MODEL), lambda i: (i, 0)),
            pl.BlockSpec((tm // (LANES // (ROPE_DIM // 2)), LANES), lambda i: (i, 0)),
            pl.BlockSpec((1, D_MODEL), const),
            pl.BlockSpec((1, LANES), const),
            pl.BlockSpec((CONV_WIDTH, PROJ_TN), const),
            pl.BlockSpec((1, PROJ_TN), const),
            pl.BlockSpec((D_MODEL, SEG_MO * PROJ_TN), const, pipeline_mode=single),
            pl.BlockSpec((D_MODEL, (N_SEG - SEG_MO) * PROJ_TN), const, pipeline_mode=single),
            pl.BlockSpec((D_MODEL, LANES), const, pipeline_mode=single),
        ],
        out_specs=[
            pl.BlockSpec((tm, N_SEG * PROJ_TN), lambda i: (i, 0)),
            pl.BlockSpec((tm, LANES), lambda i: (i, 0)),
        ],
        out_shape=[
            jax.ShapeDtypeStruct((n, N_SEG * PROJ_TN), BF16),
            jax.ShapeDtypeStruct((n, LANES), F32),
        ],
        scratch_shapes=[
            pltpu.VMEM((tm + SUBLANES, PROJ_TN), F32),
            pltpu.VMEM((tm, D_MODEL), BF16),
        ],
        compiler_params=pltpu.CompilerParams(
            dimension_semantics=("arbitrary",),
            vmem_limit_bytes=VMEM_LIMIT_BYTES),
        name="inproj",
    )(x2, pos2, nw, invf, conv_w, conv_b, wa, wb, wg)


def _mixers_kernel(lam_ref, sw_ref, q_ref, k_ref, v_ref, mqk_ref, mv_ref, mso_ref, mg_ref,
                   mgb_ref, mnw_ref, o_ref, ob_ref, vext_sc, sa_sc, sb_sc, pa_sc, pb_sc, acc_sc,
                   m_sc, cext_sc, mm_sc, mask_sc, *, seq, lambda_init):
    ml_init, ml_gate, ml_head = _mlstm_parts(mqk_ref, mv_ref, mso_ref, mg_ref, mgb_ref, mnw_ref,
                                             ob_ref, cext_sc, mm_sc, mask_sc)

    @pl.when(pl.program_id(1) == 0)
    def _():
        ml_init()

    t = ATT_T
    hd = DA_HEAD_DIM
    dv = DA_V_DIM
    nq = seq // t
    rows = 2 * t

    @pl.when((pl.program_id(0) == 0) & (pl.program_id(1) == 0))
    def _():
        vext_sc[:, dv:] = jnp.ones((seq, LANES), BF16)

    vext_sc[:, :dv] = v_ref[...]

    lp = lam_ref[...]
    lam = (jnp.exp(jnp.sum(lp[0:1] * lp[1:2], axis=-1, keepdims=True))
           - jnp.exp(jnp.sum(lp[2:3] * lp[3:4], axis=-1, keepdims=True)) + lambda_init)

    def blk(i):
        return pl.ds(i * t if isinstance(i, int) else pl.multiple_of(i * t, t), t)

    def split_q(qi):
        q = q_ref[blk(qi), :]
        lane = lax.broadcasted_iota(jnp.int32, (t, LANES), 1)
        zero = jnp.zeros_like(q)
        return jnp.where(lane < hd, q, zero), jnp.where(lane >= hd, q, zero)

    def lane_tile_max(s):
        pm = s[:, 0:LANES]
        for c in range(1, s.shape[1] // LANES):
            pm = jnp.maximum(pm, s[:, c * LANES:(c + 1) * LANES])
        return pm

    def exp_tiles(src, r0, r1, width, m):
        return jnp.concatenate(
            [jnp.exp2(src[r0:r1, c * LANES:(c + 1) * LANES] - m).astype(BF16)
             for c in range(width // LANES)], axis=1)

    def scores(dst_s, dst_pm, qi, kblk):
        q0, q1 = split_q(qi)
        s = _dot_nt(jnp.concatenate([q0, q1], axis=0), k_ref[blk(kblk), :])
        dst_s[...] = s
        dst_pm[...] = lane_tile_max(s)

    def process(src_s, src_pm, qi, vblk):
        m_prev = m_sc[qi]
        m_new = jnp.maximum(m_prev, jnp.max(src_pm[...], axis=-1, keepdims=True))
        alpha = jnp.exp2(m_prev - m_new)
        p = exp_tiles(src_s, 0, rows, t, m_new)
        pv = _dot(p, vext_sc[blk(vblk), :])
        acc_sc[qi] = jnp.concatenate([alpha, alpha], axis=1) * acc_sc[qi] + pv
        m_sc[qi] = m_new

    hh = t // 2

    def scores_diag(dst_s, dst_pm, qi):
        q0, q1 = split_q(qi)
        k_all = k_ref[qi * t:(qi + 1) * t, :]
        s_a = _dot_nt(jnp.concatenate([q0[:hh], q1[:hh]], axis=0), k_all[:hh])
        s_b = _dot_nt(jnp.concatenate([q0[hh:], q1[hh:]], axis=0), k_all)
        keep_a = (lax.broadcasted_iota(jnp.int32, (hh, hh), 1)
                  <= lax.broadcasted_iota(jnp.int32, (hh, hh), 0))
        keep_b = (lax.broadcasted_iota(jnp.int32, (hh, t), 1)
                  <= lax.broadcasted_iota(jnp.int32, (hh, t), 0) + hh)
        s_a = jnp.where(jnp.concatenate([keep_a, keep_a], axis=0), s_a, NEG_BIG)
        s_b = jnp.where(jnp.concatenate([keep_b, keep_b], axis=0), s_b, NEG_BIG)
        dst_s[0:t, 0:hh] = s_a
        dst_s[t:rows, :] = s_b
        dst_pm[0:t, :] = lane_tile_max(s_a)
        dst_pm[t:rows, :] = lane_tile_max(s_b)

    def process_diag(src_s, src_pm, qi):
        m_a = jnp.broadcast_to(jnp.max(src_pm[0:t, :], axis=-1, keepdims=True), (t, LANES))
        m_b = jnp.broadcast_to(jnp.max(src_pm[t:rows, :], axis=-1, keepdims=True), (t, LANES))
        pv_a = _dot(exp_tiles(src_s, 0, t, hh, m_a), vext_sc[qi * t:qi * t + hh, :])
        pv_b = _dot(exp_tiles(src_s, t, rows, t, m_b), vext_sc[qi * t:(qi + 1) * t, :])
        for dst, val_acc, val_m in ((0, pv_a[:hh], m_a[:hh]), (hh, pv_b[:hh], m_b[:hh]),
                                    (t, pv_a[hh:], m_a[hh:]), (t + hh, pv_b[hh:], m_b[hh:])):
            acc_sc[qi, dst:dst + hh, :] = val_acc
            m_sc[qi, dst:dst + hh, :] = val_m

    def finalize(qi):
        acc = acc_sc[qi]
        o = acc[:, :dv] / acc[:, dv:]
        od = o[:t] - lam * o[t:]
        y = _rms(od, sw_ref[...]) * (1.0 - lambda_init)
        o_ref[qi * t:(qi + 1) * t, :] = y.astype(BF16)

    bufs = ((sa_sc, pa_sc), (sb_sc, pb_sc))
    steps = []
    for qi in range(nq):
        steps.append((qi, qi, True))
        steps.extend((qi, j, False) for j in range(qi))
    def emit_scores(g):
        qi, kblk, diag = steps[g]
        if diag:
            scores_diag(*bufs[g % 2], qi)
        else:
            scores(*bufs[g % 2], qi, kblk)

    emit_scores(0)

    def emit(g0, g1, ml_chunk):
        ctx = None
        heads_done = 0
        every = max(1, (g1 - g0) // (ML_HEADS + 1))
        for g in range(g0, g1):
            if ml_chunk is not None and (g - g0) % every == 0:
                if ctx is None:
                    ctx = ml_gate(ml_chunk * ML_CHUNK)
                elif heads_done < ML_HEADS:
                    ml_head(ctx, heads_done)
                    heads_done += 1
            qi, kblk, diag = steps[g]
            if g + 1 < len(steps):
                emit_scores(g + 1)
            if diag:
                process_diag(*bufs[g % 2], qi)
            else:
                process(*bufs[g % 2], qi, kblk)
            if g + 1 == len(steps) or steps[g + 1][0] != qi:
                finalize(qi)
        if ml_chunk is not None:
            for h in range(heads_done, ML_HEADS):
                ml_head(ctx, h)

    one = jnp.minimum(pl.program_id(0), 0) + 1
    starts = list(range(0, len(steps), ATT_STEPS_PER_REGION))
    ml_chunks = t // ML_CHUNK
    assert len(starts) % ml_chunks == 0
    for r, g0 in enumerate(starts):
        g1 = min(g0 + ATT_STEPS_PER_REGION, len(steps))
        stride = len(starts) // ml_chunks
        ml_chunk = r // stride if r % stride == 0 else None

        def region(_, c, g0=g0, g1=g1, ml_chunk=ml_chunk):
            emit(g0, g1, ml_chunk)
            return c

        lax.fori_loop(0, one, region, 0)


def _mixers(proj, gates, lam_p, subln_w, gate_b, ml_norm_w, batch, seq, lambda_init):
    n = proj.shape[0]
    t = ATT_T
    nq = seq // t
    assert nq == DA_HEADS and t % ML_CHUNK == 0
    kern = functools.partial(_mixers_kernel, seq=seq, lambda_init=lambda_init)
    hb = PROJ_TN // LANES
    mw = ML_HEADS * ML_V_DIM
    const = lambda b, h: (0, 0)
    return pl.pallas_call(
        kern,
        grid=(batch, DA_HEADS),
        in_specs=[
            pl.BlockSpec((4, DA_HEAD_DIM), const),
            pl.BlockSpec((1, DA_V_DIM), const),
            pl.BlockSpec((seq, LANES), lambda b, h: (b, SEG_Q * hb + h)),
            pl.BlockSpec((seq, LANES), lambda b, h: (b, SEG_K * hb + h)),
            pl.BlockSpec((seq, LANES), lambda b, h: (b, SEG_V * hb + h)),
            pl.BlockSpec((t, PROJ_TN), lambda b, h: (b * nq + h, SEG_MQK)),
            pl.BlockSpec((t, PROJ_TN), lambda b, h: (b * nq + h, SEG_MV)),
            pl.BlockSpec((t, PROJ_TN), lambda b, h: (b * nq + h, SEG_MO)),
            pl.BlockSpec((t, LANES), lambda b, h: (b * nq + h, 0)),
            pl.BlockSpec((1, LANES), const),
            pl.BlockSpec((1, mw), const),
        ],
        out_specs=[
            pl.BlockSpec((seq, LANES), lambda b, h: (b, h)),
            pl.BlockSpec((t, mw), lambda b, h: (b * nq + h, 0)),
        ],
        out_shape=[
            jax.ShapeDtypeStruct((n, DA_HEADS * DA_V_DIM), BF16),
            jax.ShapeDtypeStruct((n, mw), BF16),
        ],
        scratch_shapes=[
            pltpu.VMEM((seq, 2 * LANES), BF16),
            pltpu.VMEM((2 * t, t), F32),
            pltpu.VMEM((2 * t, t), F32),
            pltpu.VMEM((2 * t, LANES), F32),
            pltpu.VMEM((2 * t, LANES), F32),
            pltpu.VMEM((nq, 2 * t, 2 * LANES), F32),
            pltpu.VMEM((nq, 2 * t, LANES), F32),
            pltpu.VMEM((ML_HEADS, ML_QK_DIM, 2 * LANES), F32),
            pltpu.VMEM((SUBLANES, LANES), F32),
            pltpu.VMEM((ML_CHUNK, ML_CHUNK), F32),
        ],
        compiler_params=pltpu.CompilerParams(
            dimension_semantics=("arbitrary", "arbitrary"),
            vmem_limit_bytes=VMEM_LIMIT_BYTES),
        name="mixers",
    )(lam_p, subln_w, proj, proj, proj, proj, proj, proj, gates, gate_b, ml_norm_w)


def _mlstm_parts(qk_ref, v_ref, so_ref, g_ref, gb_ref, nw_ref, out_ref, cext_sc, m_sc, mask_sc):
    L = ML_CHUNK
    dk, dv = ML_QK_DIM, ML_V_DIM
    qkw = ML_HEADS * dk
    nlt = L // LANES

    def init():
        cext_sc[...] = jnp.zeros(cext_sc.shape, F32)
        m_sc[...] = jnp.zeros(m_sc.shape, F32)
        row = lax.broadcasted_iota(jnp.int32, (L, L), 0)
        col = lax.broadcasted_iota(jnp.int32, (L, L), 1)
        mask_sc[...] = jnp.where(col <= row, 0.0, NEG_BIG)

    def gate(r0):
        rs = slice(r0, r0 + L)
        tri_b = (mask_sc[...] == 0.0).astype(BF16)
        rowi = lax.broadcasted_iota(jnp.int32, (L, LANES), 0)
        g = g_ref[rs, :] + gb_ref[...]
        logf = jnp.minimum(g, 0.0) - jnp.log(1.0 + jnp.exp(-jnp.abs(g)))
        hi = logf.astype(BF16)
        r1 = logf - hi.astype(F32)
        mid = r1.astype(BF16)
        lo = (r1 - mid.astype(F32)).astype(BF16)
        bcs = _dot(tri_b, hi) + _dot(tri_b, mid) + _dot(tri_b, lo)
        b_al = pltpu.roll(bcs, LANES - ML_HEADS, 1)
        a = g - b_al
        m_prev = m_sc[0:1, :]
        cm = a
        d = 1
        while d < L:
            cm = jnp.maximum(cm, jnp.where(rowi >= d, pltpu.roll(cm, d, 0), NEG_BIG))
            d *= 2
        u = jnp.maximum(cm, m_prev)
        b_last = b_al[L - 1:L, :]
        w_log = b_last + a
        m_new = jnp.maximum(b_last + m_prev, jnp.max(w_log, axis=0, keepdims=True))
        m_sc[0:1, :] = m_new
        return dict(rs=rs, u=u, mt=b_al + u, m_prev=m_prev,
                    decay=jnp.exp(b_last + m_prev - m_new),
                    a_t=a.T,
                    ws_t=jnp.exp(w_log - m_new).T)

    def head(ctx, h):
        rs = ctx["rs"]
        u_b = jnp.broadcast_to(ctx["u"][:, h:h + 1], (L, LANES))
        mt_b = jnp.broadcast_to(ctx["mt"][:, h:h + 1], (L, LANES))
        inter_b = jnp.exp(ctx["m_prev"][:, h:h + 1] - u_b)
        floor_b = jnp.exp(-mt_b)
        dexp = jnp.concatenate(
            [jnp.exp(ctx["a_t"][h:h + 1, t * LANES:(t + 1) * LANES] - u_b
                     + mask_sc[:, t * LANES:(t + 1) * LANES]) for t in range(nlt)], axis=1)
        qh = qk_ref[rs, h * dk:(h + 1) * dk]
        kh_t = qk_ref[rs, qkw + h * dk:qkw + (h + 1) * dk].T
        cext = cext_sc[h]
        vext = jnp.concatenate([v_ref[rs, h * dv:(h + 1) * dv], jnp.ones((L, LANES), BF16)],
                               axis=1)
        s = (_dot(qh, kh_t) * dexp).astype(BF16)
        hext = (_dot(s, vext)
                + jnp.concatenate([inter_b, inter_b], axis=1) * _dot(qh, cext.astype(BF16)))
        hm = hext[:, :dv] / jnp.maximum(jnp.abs(hext[:, dv:]), floor_b)
        y = _rms(hm, nw_ref[:, h * dv:(h + 1) * dv])
        y = so_ref[rs, h * dv:(h + 1) * dv].astype(F32) * y
        out_ref[rs, h * dv:(h + 1) * dv] = y.astype(BF16)
        kw_t = (kh_t.astype(F32) * ctx["ws_t"][h:h + 1, :]).astype(BF16)
        cext_sc[h] = ctx["decay"][:, h:h + 1] * cext + _dot(kw_t, vext)

    return init, gate, head


def _post_kernel(x_ref, oa_ref, ob_ref, g0_ref, g1_ref, wa_ref, wb_ref, wo_ref, nf_ref,
                 w1_ref, w2_ref, fw_ref, out_ref, *, final_norm):
    ya = _dot(oa_ref[...], wa_ref[...])
    yb = _dot(ob_ref[...], wb_ref[...])
    merged = g0_ref[...].astype(F32) * ya + g1_ref[...].astype(F32) * yb
    h = x_ref[...] + _dot(merged.astype(BF16), wo_ref[...])
    hn = _rms(h, nf_ref[...]).astype(BF16)
    acc = h
    for c0 in range(0, D_FF, FF_CHUNK):
        u = jnp.maximum(_dot(hn, w1_ref[:, c0:c0 + FF_CHUNK]), 0.0)
        acc = acc + _dot((u * u).astype(BF16), w2_ref[c0:c0 + FF_CHUNK, :])
    if final_norm:
        acc = _rms(acc, fw_ref[...])
    out_ref[...] = acc


def _post(x2, oa, ob, proj, wa, wb, wo, nf, w1, w2, fw, final_norm):
    n = x2.shape[0]
    tm = POST_TM
    const = lambda i: (0, 0)
    single = pl.Buffered(1)
    kern = functools.partial(_post_kernel, final_norm=final_norm)
    return pl.pallas_call(
        kern,
        grid=(n // tm,),
        in_specs=[
            pl.BlockSpec((tm, D_MODEL), lambda i: (i, 0)),
            pl.BlockSpec((tm, D_MODEL), lambda i: (i, 0)),
            pl.BlockSpec((tm, D_MODEL), lambda i: (i, 0)),
            pl.BlockSpec((tm, PROJ_TN), lambda i: (i, SEG_G0)),
            pl.BlockSpec((tm, PROJ_TN), lambda i: (i, SEG_G1)),
            pl.BlockSpec((D_MODEL, D_MODEL), const, pipeline_mode=single),
            pl.BlockSpec((D_MODEL, D_MODEL), const, pipeline_mode=single),
            pl.BlockSpec((D_MODEL, D_MODEL), const, pipeline_mode=single),
            pl.BlockSpec((1, D_MODEL), const),
            pl.BlockSpec((D_MODEL, D_FF), const, pipeline_mode=single),
            pl.BlockSpec((D_FF, D_MODEL), const, pipeline_mode=single),
            pl.BlockSpec((1, D_MODEL), const),
        ],
        out_specs=pl.BlockSpec((tm, D_MODEL), lambda i: (i, 0)),
        out_shape=jax.ShapeDtypeStruct((n, D_MODEL), F32),
        compiler_params=pltpu.CompilerParams(
            dimension_semantics=("arbitrary",),
            vmem_limit_bytes=VMEM_LIMIT_BYTES),
        name="post_mixer",
    )(x2, oa, ob, proj, proj, wa, wb, wo, nf, w1, w2, fw)


def kernel(x, positions, norm_mix_w, w_in, ml_gate_b, conv_w, conv_b, da_lambda, da_subln_w,
           ml_norm_w, w_proj_a, w_proj_b, w_out, norm_ffn_w, w_ff1, w_ff2, final_norm_w):
    batch, seq, _ = x.shape
    n = batch * seq
    depth = w_in.shape[0]
    assert seq % ATT_T == 0 and seq % PROJ_TM == 0 and n % POST_TM == 0

    da_w = DA_HEADS * 2 * DA_HEAD_DIM
    ml_qk = ML_HEADS * ML_QK_DIM
    ml_v = ML_HEADS * ML_V_DIM
    o_mq = 3 * da_w
    o_mv = o_mq + 2 * ml_qk
    o_gi = o_mv + ml_v
    o_mo = o_gi + 2 * ML_HEADS
    o_gate = o_mo + ml_v

    pos2 = jnp.repeat(positions.reshape(n), ROPE_DIM // 2).reshape(-1, LANES)
    inv = ROPE_THETA ** (-jnp.arange(0, ROPE_DIM, 2, dtype=F32) / ROPE_DIM)
    invf = jnp.tile(inv, LANES // (ROPE_DIM // 2)).reshape(1, LANES)

    h = x.reshape(n, D_MODEL)
    for l in range(depth):
        lambda_init = 0.8 - 0.6 * math.exp(-0.3 * l)
        w = w_in[l].astype(BF16)
        wa = w
        wb = w[:, o_mo:]
        wg = jnp.pad(w[:, o_gi:o_mo], ((0, 0), (0, LANES - 2 * ML_HEADS)))
        gate_b = jnp.pad(ml_gate_b[l], (0, LANES - 2 * ML_HEADS)).reshape(1, LANES)

        proj, gates = _inproj(h, pos2, norm_mix_w[l].reshape(1, D_MODEL), invf, conv_w[l],
                              conv_b[l].reshape(1, -1), wa, wb, wg, seq)
        oa, ob = _mixers(proj, gates, da_lambda[l], da_subln_w[l].reshape(1, DA_V_DIM), gate_b,
                         ml_norm_w[l].reshape(1, -1), batch, seq, lambda_init)
        h = _post(h, oa, ob, proj,
                  w_proj_a[l].astype(BF16), w_proj_b[l].astype(BF16), w_out[l].astype(BF16),
                  norm_ffn_w[l].reshape(1, D_MODEL), w_ff1[l].astype(BF16),
                  w_ff2[l].astype(BF16), final_norm_w.reshape(1, D_MODEL),
                  final_norm=(l == depth - 1))
    return h.reshape(batch, seq, D_MODEL)
```

```python
import functools
import math

import jax
import jax.numpy as jnp
from jax import lax
from jax.experimental import pallas as pl
from jax.experimental.pallas import tpu as pltpu

F32 = jnp.float32
BF16 = jnp.bfloat16

D_MODEL = 1024
DA_HEADS = 8
DA_HEAD_DIM = 64
DA_V_DIM = 128
ROPE_DIM = 16
ROPE_THETA = 500000.0
ML_HEADS = 8
ML_QK_DIM = 64
ML_V_DIM = 128
CONV_WIDTH = 4
D_FF = 4 * D_MODEL
EPS = 1e-6

LANES = 128
SUBLANES = 8
VMEM_LIMIT_BYTES = 56 * 1024 * 1024

PROJ_TM = 512
PROJ_TN = 1024
PROJ_CN = 256
ATT_T = 512
ATT_STEPS_PER_REGION = 9
ML_CHUNK = 256
POST_TM = 512
FF_CHUNK = 1024

SEG_Q, SEG_K, SEG_V, SEG_MQK, SEG_MV, SEG_MO, SEG_G0, SEG_G1 = range(8)
N_SEG = 8

NEG_BIG = -1e30
LOG2E = 1.4426950408889634


def _dot(a, b):
    return jnp.dot(a, b, preferred_element_type=F32)


def _dot_nt(a, b):
    return lax.dot_general(a, b, (((1,), (1,)), ((), ())), preferred_element_type=F32)


def _sigmoid(x):
    return 0.5 * jnp.tanh(0.5 * x) + 0.5


def _rms(x, w):
    return x * lax.rsqrt(jnp.mean(x * x, axis=-1, keepdims=True) + EPS) * w


def _inproj_kernel(x_ref, posd_ref, nw_ref, invf_ref, cw_ref, cb_ref, wa_ref, wb_ref, wg_ref,
                   out_ref, gates_ref, cbuf, xn_sc, *, tiles_per_seq):
    tm = PROJ_TM
    cn = PROJ_CN
    i = pl.program_id(0)

    hist_rows = CONV_WIDTH - 1
    first = i % tiles_per_seq == 0

    @pl.when(first)
    def _():
        cbuf[0:SUBLANES, :] = jnp.zeros((SUBLANES, PROJ_TN), F32)

    @pl.when(jnp.logical_not(first))
    def _():
        cbuf[0:SUBLANES, :] = cbuf[tm:tm + SUBLANES, :]

    xn_sc[...] = _rms(x_ref[...], nw_ref[...]).astype(BF16)
    gates_ref[...] = _dot(xn_sc[...], wg_ref[...])
    half = ROPE_DIM // 2

    def chunk(seg, c_in_seg, tables):
        c0 = seg * PROJ_TN + c_in_seg
        if seg < SEG_MO:
            w_chunk = wa_ref[:, c0:c0 + cn]
        else:
            w_chunk = wb_ref[:, c0 - SEG_MO * PROJ_TN:c0 - SEG_MO * PROJ_TN + cn]
        acc = _dot(xn_sc[...], w_chunk)
        if seg in (SEG_Q, SEG_K):
            cos, sin_signed, partner_lane = tables
            scale = DA_HEAD_DIM ** -0.5 * LOG2E if seg == SEG_Q else 1.0
            for l0 in range(0, cn, LANES):
                xc = acc[:, l0:l0 + LANES]
                r = xc * cos + jnp.take_along_axis(xc, partner_lane, axis=1) * sin_signed
                out_ref[:, c0 + l0:c0 + l0 + LANES] = (r * scale).astype(BF16)
        elif seg == SEG_MQK:
            m0 = c0 - SEG_MQK * PROJ_TN
            cbuf[SUBLANES:SUBLANES + tm, m0:m0 + cn] = acc
            conv = cb_ref[:, m0:m0 + cn]
            for j in range(CONV_WIDTH):
                off = SUBLANES - hist_rows + j
                conv = conv + (cw_ref[j:j + 1, m0:m0 + cn]
                               * cbuf[off:off + tm, m0:m0 + cn])
            y = conv * _sigmoid(conv)
            if m0 < ML_HEADS * ML_QK_DIM:
                y = y * (ML_QK_DIM ** -0.5)
            out_ref[:, c0:c0 + cn] = y.astype(BF16)
        elif seg in (SEG_V, SEG_MV):
            out_ref[:, c0:c0 + cn] = acc.astype(BF16)
        else:
            out_ref[:, c0:c0 + cn] = _sigmoid(acc).astype(BF16)

    rpd = LANES // half
    ang = posd_ref[...].astype(F32) * invf_ref[...]
    lane_id = lax.broadcasted_iota(jnp.int32, (tm, LANES), 1)
    row_id = lax.broadcasted_iota(jnp.int32, (tm, LANES), 0)
    own = (lane_id // half) == (row_id % rpd)
    expand = ((lax.broadcasted_iota(jnp.int32, (LANES, LANES), 0) % half)
              == (lax.broadcasted_iota(jnp.int32, (LANES, LANES), 1) % half)).astype(BF16)

    def expand_rows(d):
        rep = jnp.concatenate([jnp.broadcast_to(d[r:r + 1, :], (rpd, LANES))
                               for r in range(tm // rpd)], axis=0)
        a = jnp.where(own, rep, 0.0)
        hi = a.astype(BF16)
        lo = (a - hi.astype(F32)).astype(BF16)
        return _dot(hi, expand) + _dot(lo, expand)

    c = expand_rows(jnp.cos(ang))
    s = expand_rows(jnp.sin(ang))
    sub = lane_id % DA_HEAD_DIM
    partner_lane = jnp.where(sub < half, lane_id + half,
                             jnp.where(sub < ROPE_DIM, lane_id - half, lane_id))
    tables = (jnp.where(sub < ROPE_DIM, c, 1.0),
              jnp.where(sub < half, -s, jnp.where(sub < ROPE_DIM, s, 0.0)), partner_lane)
    for c_in_seg in range(0, PROJ_TN, cn):
        for seg in (SEG_V, SEG_Q, SEG_MV, SEG_MQK, SEG_G0, SEG_K, SEG_MO, SEG_G1):
            chunk(seg, c_in_seg, tables)


def _inproj(x2, pos2, nw, invf, conv_w, conv_b, wa, wb, wg, seq):
    n = x2.shape[0]
    tm = PROJ_TM
    const = lambda i: (0, 0)
    single = pl.Buffered(1)
    kern = functools.partial(_inproj_kernel, tiles_per_seq=seq // tm)
    return pl.pallas_call(
        kern,
        grid=(n // tm,),
        in_specs=[
            pl.BlockSpec((tm, D_MODEL), lambda i: (i, 0)),
            pl.BlockSpec((tm // (LANES // (ROPE_DIM // 2)), LANES), lambda i: (i, 0)),
            pl.BlockSpec((1, D_MODEL), const),
            pl.BlockSpec((1, LANES), const),
            pl.BlockSpec((CONV_WIDTH, PROJ_TN), const),
            pl.BlockSpec((1, PROJ_TN), const),
            pl.BlockSpec((D_MODEL, SEG_MO * PROJ_TN), const, pipeline_mode=single),
            pl.BlockSpec((D_MODEL, (N_SEG - SEG_MO) * PROJ_TN), const, pipeline_mode=single),
            pl.BlockSpec((D_MODEL, LANES), const, pipeline_mode=single),
        ],
        out_specs=[
            pl.BlockSpec((tm, N_SEG * PROJ_TN), lambda i: (i, 0)),
            pl.BlockSpec((tm, LANES), lambda i: (i, 0)),
        ],
        out_shape=[
            jax.ShapeDtypeStruct((n, N_SEG * PROJ_TN), BF16),
            jax.ShapeDtypeStruct((n, LANES), F32),
        ],
        scratch_shapes=[
            pltpu.VMEM((tm + SUBLANES, PROJ_TN), F32),
            pltpu.VMEM((tm, D_MODEL), BF16),
        ],
        compiler_params=pltpu.CompilerParams(
            dimension_semantics=("arbitrary",),
            vmem_limit_bytes=VMEM_LIMIT_BYTES),
        name="inproj",
    )(x2, pos2, nw, invf, conv_w, conv_b, wa, wb, wg)


def _mixers_kernel(lam_ref, sw_ref, q_ref, k_ref, v_ref, mqk_ref, mv_ref, mso_ref, mg_ref,
                   mgb_ref, mnw_ref, o_ref, ob_ref, vext_sc, sa_sc, sb_sc, pa_sc, pb_sc, acc_sc,
                   m_sc, cext_sc, mm_sc, mask_sc, *, seq, lambda_init):
    ml_init, ml_gate, ml_head = _mlstm_parts(mqk_ref, mv_ref, mso_ref, mg_ref, mgb_ref, mnw_ref,
                                             ob_ref, cext_sc, mm_sc, mask_sc)

    @pl.when(pl.program_id(1) == 0)
    def _():
        ml_init()

    t = ATT_T
    hd = DA_HEAD_DIM
    dv = DA_V_DIM
    nq = seq // t
    rows = 2 * t

    @pl.when((pl.program_id(0) == 0) & (pl.program_id(1) == 0))
    def _():
        vext_sc[:, dv:] = jnp.ones((seq, LANES), BF16)

    vext_sc[:, :dv] = v_ref[...]

    lp = lam_ref[...]
    lam = (jnp.exp(jnp.sum(lp[0:1] * lp[1:2], axis=-1, keepdims=True))
           - jnp.exp(jnp.sum(lp[2:3] * lp[3:4], axis=-1, keepdims=True)) + lambda_init)

    def blk(i):
        return pl.ds(i * t if isinstance(i, int) else pl.multiple_of(i * t, t), t)

    def split_q(qi):
        q = q_ref[blk(qi), :]
        lane = lax.broadcasted_iota(jnp.int32, (t, LANES), 1)
        zero = jnp.zeros_like(q)
        return jnp.where(lane < hd, q, zero), jnp.where(lane >= hd, q, zero)

    def lane_tile_max(s):
        pm = s[:, 0:LANES]
        for c in range(1, s.shape[1] // LANES):
            pm = jnp.maximum(pm, s[:, c * LANES:(c + 1) * LANES])
        return pm

    def exp_tiles(src, r0, r1, width, m):
        return jnp.concatenate(
            [jnp.exp2(src[r0:r1, c * LANES:(c + 1) * LANES] - m).astype(BF16)
             for c in range(width // LANES)], axis=1)

    def scores(dst_s, dst_pm, qi, kblk):
        q0, q1 = split_q(qi)
        s = _dot_nt(jnp.concatenate([q0, q1], axis=0), k_ref[blk(kblk), :])
        dst_s[...] = s
        dst_pm[...] = lane_tile_max(s)

    def process(src_s, src_pm, qi, vblk):
        m_prev = m_sc[qi]
        m_new = jnp.maximum(m_prev, jnp.max(src_pm[...], axis=-1, keepdims=True))
        alpha = jnp.exp2(m_prev - m_new)
        p = exp_tiles(src_s, 0, rows, t, m_new)
        pv = _dot(p, vext_sc[blk(vblk), :])
        acc_sc[qi] = jnp.concatenate([alpha, alpha], axis=1) * acc_sc[qi] + pv
        m_sc[qi] = m_new

    hh = t // 2

    def scores_diag(dst_s, dst_pm, qi):
        q0, q1 = split_q(qi)
        k_all = k_ref[qi * t:(qi + 1) * t, :]
        s_a = _dot_nt(jnp.concatenate([q0[:hh], q1[:hh]], axis=0), k_all[:hh])
        s_b = _dot_nt(jnp.concatenate([q0[hh:], q1[hh:]], axis=0), k_all)
        keep_a = (lax.broadcasted_iota(jnp.int32, (hh, hh), 1)
                  <= lax.broadcasted_iota(jnp.int32, (hh, hh), 0))
        keep_b = (lax.broadcasted_iota(jnp.int32, (hh, t), 1)
                  <= lax.broadcasted_iota(jnp.int32, (hh, t), 0) + hh)
        s_a = jnp.where(jnp.concatenate([keep_a, keep_a], axis=0), s_a, NEG_BIG)
        s_b = jnp.where(jnp.concatenate([keep_b, keep_b], axis=0), s_b, NEG_BIG)
        dst_s[0:t, 0:hh] = s_a
        dst_s[t:rows, :] = s_b
        dst_pm[0:t, :] = lane_tile_max(s_a)
        dst_pm[t:rows, :] = lane_tile_max(s_b)

    def process_diag(src_s, src_pm, qi):
        m_a = jnp.broadcast_to(jnp.max(src_pm[0:t, :], axis=-1, keepdims=True), (t, LANES))
        m_b = jnp.broadcast_to(jnp.max(src_pm[t:rows, :], axis=-1, keepdims=True), (t, LANES))
        pv_a = _dot(exp_tiles(src_s, 0, t, hh, m_a), vext_sc[qi * t:qi * t + hh, :])
        pv_b = _dot(exp_tiles(src_s, t, rows, t, m_b), vext_sc[qi * t:(qi + 1) * t, :])
        for dst, val_acc, val_m in ((0, pv_a[:hh], m_a[:hh]), (hh, pv_b[:hh], m_b[:hh]),
                                    (t, pv_a[hh:], m_a[hh:]), (t + hh, pv_b[hh:], m_b[hh:])):
            acc_sc[qi, dst:dst + hh, :] = val_acc
            m_sc[qi, dst:dst + hh, :] = val_m

    def finalize(qi):
        acc = acc_sc[qi]
        o = acc[:, :dv] / acc[:, dv:]
        od = o[:t] - lam * o[t:]
        y = _rms(od, sw_ref[...]) * (1.0 - lambda_init)
        o_ref[qi * t:(qi + 1) * t, :] = y.astype(BF16)

    bufs = ((sa_sc, pa_sc), (sb_sc, pb_sc))
    steps = []
    for qi in range(nq):
        steps.append((qi, qi, True))
        steps.extend((qi, j, False) for j in range(qi))
    def emit_scores(g):
        qi, kblk, diag = steps[g]
        if diag:
            scores_diag(*bufs[g % 2], qi)
        else:
            scores(*bufs[g % 2], qi, kblk)

    emit_scores(0)

    def emit(g0, g1, ml_chunk):
        ctx = None
        heads_done = 0
        every = max(1, (g1 - g0) // (ML_HEADS + 1))
        for g in range(g0, g1):
            if ml_chunk is not None and (g - g0) % every == 0:
                if ctx is None:
                    ctx = ml_gate(ml_chunk * ML_CHUNK)
                elif heads_done < ML_HEADS:
                    ml_head(ctx, heads_done)
                    heads_done += 1
            qi, kblk, diag = steps[g]
            if g + 1 < len(steps):
                emit_scores(g + 1)
            if diag:
                process_diag(*bufs[g % 2], qi)
            else:
                process(*bufs[g % 2], qi, kblk)
            if g + 1 == len(steps) or steps[g + 1][0] != qi:
                finalize(qi)
        if ml_chunk is not None:
            for h in range(heads_done, ML_HEADS):
                ml_head(ctx, h)

    one = jnp.minimum(pl.program_id(0), 0) + 1
    starts = list(range(0, len(steps), ATT_STEPS_PER_REGION))
    ml_chunks = t // ML_CHUNK
    assert len(starts) % ml_chunks == 0
    for r, g0 in enumerate(starts):
        g1 = min(g0 + ATT_STEPS_PER_REGION, len(steps))
        stride = len(starts) // ml_chunks
        ml_chunk = r // stride if r % stride == 0 else None

        def region(_, c, g0=g0, g1=g1, ml_chunk=ml_chunk):
            emit(g0, g1, ml_chunk)
            return c

        lax.fori_loop(0, one, region, 0)


def _mixers(proj, gates, lam_p, subln_w, gate_b, ml_norm_w, batch, seq, lambda_init):
    n = proj.shape[0]
    t = ATT_T
    nq = seq // t
    assert nq == DA_HEADS and t % ML_CHUNK == 0
    kern = functools.partial(_mixers_kernel, seq=seq, lambda_init=lambda_init)
    hb = PROJ_TN // LANES
    mw = ML_HEADS * ML_V_DIM
    const = lambda b, h: (0, 0)
    return pl.pallas_call(
        kern,
        grid=(batch, DA_HEADS),
        in_specs=[
            pl.BlockSpec((4, DA_HEAD_DIM), const),
            pl.BlockSpec((1, DA_V_DIM), const),
            pl.BlockSpec((seq, LANES), lambda b, h: (b, SEG_Q * hb + h)),
            pl.BlockSpec((seq, LANES), lambda b, h: (b, SEG_K * hb + h)),
            pl.BlockSpec((seq, LANES), lambda b, h: (b, SEG_V * hb + h)),
            pl.BlockSpec((t, PROJ_TN), lambda b, h: (b * nq + h, SEG_MQK)),
            pl.BlockSpec((t, PROJ_TN), lambda b, h: (b * nq + h, SEG_MV)),
            pl.BlockSpec((t, PROJ_TN), lambda b, h: (b * nq + h, SEG_MO)),
            pl.BlockSpec((t, LANES), lambda b, h: (b * nq + h, 0)),
            pl.BlockSpec((1, LANES), const),
            pl.BlockSpec((1, mw), const),
        ],
        out_specs=[
            pl.BlockSpec((seq, LANES), lambda b, h: (b, h)),
            pl.BlockSpec((t, mw), lambda b, h: (b * nq + h, 0)),
        ],
        out_shape=[
            jax.ShapeDtypeStruct((n, DA_HEADS * DA_V_DIM), BF16),
            jax.ShapeDtypeStruct((n, mw), BF16),
        ],
        scratch_shapes=[
            pltpu.VMEM((seq, 2 * LANES), BF16),
            pltpu.VMEM((2 * t, t), F32),
            pltpu.VMEM((2 * t, t), F32),
            pltpu.VMEM((2 * t, LANES), F32),
            pltpu.VMEM((2 * t, LANES), F32),
            pltpu.VMEM((nq, 2 * t, 2 * LANES), F32),
            pltpu.VMEM((nq, 2 * t, LANES), F32),
            pltpu.VMEM((ML_HEADS, ML_QK_DIM, 2 * LANES), F32),
            pltpu.VMEM((SUBLANES, LANES), F32),
            pltpu.VMEM((ML_CHUNK, ML_CHUNK), F32),
        ],
        compiler_params=pltpu.CompilerParams(
            dimension_semantics=("arbitrary", "arbitrary"),
            vmem_limit_bytes=VMEM_LIMIT_BYTES),
        name="mixers",
    )(lam_p, subln_w, proj, proj, proj, proj, proj, proj, gates, gate_b, ml_norm_w)


def _mlstm_parts(qk_ref, v_ref, so_ref, g_ref, gb_ref, nw_ref, out_ref, cext_sc, m_sc, mask_sc):
    L = ML_CHUNK
    dk, dv = ML_QK_DIM, ML_V_DIM
    qkw = ML_HEADS * dk
    nlt = L // LANES

    def init():
        cext_sc[...] = jnp.zeros(cext_sc.shape, F32)
        m_sc[...] = jnp.zeros(m_sc.shape, F32)
        row = lax.broadcasted_iota(jnp.int32, (L, L), 0)
        col = lax.broadcasted_iota(jnp.int32, (L, L), 1)
        mask_sc[...] = jnp.where(col <= row, 0.0, NEG_BIG)

    def gate(r0):
        rs = slice(r0, r0 + L)
        tri_b = (mask_sc[...] == 0.0).astype(BF16)
        rowi = lax.broadcasted_iota(jnp.int32, (L, LANES), 0)
        g = g_ref[rs, :] + gb_ref[...]
        logf = jnp.minimum(g, 0.0) - jnp.log(1.0 + jnp.exp(-jnp.abs(g)))
        hi = logf.astype(BF16)
        r1 = logf - hi.astype(F32)
        mid = r1.astype(BF16)
        lo = (r1 - mid.astype(F32)).astype(BF16)
        bcs = _dot(tri_b, hi) + _dot(tri_b, mid) + _dot(tri_b, lo)
        b_al = pltpu.roll(bcs, LANES - ML_HEADS, 1)
        a = g - b_al
        m_prev = m_sc[0:1, :]
        cm = a
        d = 1
        while d < L:
            cm = jnp.maximum(cm, jnp.where(rowi >= d, pltpu.roll(cm, d, 0), NEG_BIG))
            d *= 2
        u = jnp.maximum(cm, m_prev)
        b_last = b_al[L - 1:L, :]
        w_log = b_last + a
        m_new = jnp.maximum(b_last + m_prev, jnp.max(w_log, axis=0, keepdims=True))
        m_sc[0:1, :] = m_new
        return dict(rs=rs, u=u, mt=b_al + u, m_prev=m_prev,
                    decay=jnp.exp(b_last + m_prev - m_new),
                    a_t=a.T,
                    ws_t=jnp.exp(w_log - m_new).T)

    def head(ctx, h):
        rs = ctx["rs"]
        u_b = jnp.broadcast_to(ctx["u"][:, h:h + 1], (L, LANES))
        mt_b = jnp.broadcast_to(ctx["mt"][:, h:h + 1], (L, LANES))
        inter_b = jnp.exp(ctx["m_prev"][:, h:h + 1] - u_b)
        floor_b = jnp.exp(-mt_b)
        dexp = jnp.concatenate(
            [jnp.exp(ctx["a_t"][h:h + 1, t * LANES:(t + 1) * LANES] - u_b
                     + mask_sc[:, t * LANES:(t + 1) * LANES]) for t in range(nlt)], axis=1)
        qh = qk_ref[rs, h * dk:(h + 1) * dk]
        kh_t = qk_ref[rs, qkw + h * dk:qkw + (h + 1) * dk].T
        cext = cext_sc[h]
        vext = jnp.concatenate([v_ref[rs, h * dv:(h + 1) * dv], jnp.ones((L, LANES), BF16)],
                               axis=1)
        s = (_dot(qh, kh_t) * dexp).astype(BF16)
        hext = (_dot(s, vext)
                + jnp.concatenate([inter_b, inter_b], axis=1) * _dot(qh, cext.astype(BF16)))
        hm = hext[:, :dv] / jnp.maximum(jnp.abs(hext[:, dv:]), floor_b)
        y = _rms(hm, nw_ref[:, h * dv:(h + 1) * dv])
        y = so_ref[rs, h * dv:(h + 1) * dv].astype(F32) * y
        out_ref[rs, h * dv:(h + 1) * dv] = y.astype(BF16)
        kw_t = (kh_t.astype(F32) * ctx["ws_t"][h:h + 1, :]).astype(BF16)
        cext_sc[h] = ctx["decay"][:, h:h + 1] * cext + _dot(kw_t, vext)

    return init, gate, head


def _post_kernel(x_ref, oa_ref, ob_ref, g0_ref, g1_ref, wa_ref, wb_ref, wo_ref, nf_ref,
                 w1_ref, w2_ref, fw_ref, out_ref, *, final_norm):
    ya = _dot(oa_ref[...], wa_ref[...])
    yb = _dot(ob_ref[...], wb_ref[...])
    merged = g0_ref[...].astype(F32) * ya + g1_ref[...].astype(F32) * yb
    h = x_ref[...] + _dot(merged.astype(BF16), wo_ref[...])
    hn = _rms(h, nf_ref[...]).astype(BF16)
    acc = h
    for c0 in range(0, D_FF, FF_CHUNK):
        u = jnp.maximum(_dot(hn, w1_ref[:, c0:c0 + FF_CHUNK]), 0.0)
        acc = acc + _dot((u * u).astype(BF16), w2_ref[c0:c0 + FF_CHUNK, :])
    if final_norm:
        acc = _rms(acc, fw_ref[...])
    out_ref[...] = acc


def _post(x2, oa, ob, proj, wa, wb, wo, nf, w1, w2, fw, final_norm):
    n = x2.shape[0]
    tm = POST_TM
    const = lambda i: (0, 0)
    single = pl.Buffered(1)
    kern = functools.partial(_post_kernel, final_norm=final_norm)
    return pl.pallas_call(
        kern,
        grid=(n // tm,),
        in_specs=[
            pl.BlockSpec((tm, D_MODEL), lambda i: (i, 0)),
            pl.BlockSpec((tm, D_MODEL), lambda i: (i, 0)),
            pl.BlockSpec((tm, D_MODEL), lambda i: (i, 0)),
            pl.BlockSpec((tm, PROJ_TN), lambda i: (i, SEG_G0)),
            pl.BlockSpec((tm, PROJ_TN), lambda i: (i, SEG_G1)),
            pl.BlockSpec((D_MODEL, D_MODEL), const, pipeline_mode=single),
            pl.BlockSpec((D_MODEL, D_MODEL), const, pipeline_mode=single),
            pl.BlockSpec((D_MODEL, D_MODEL), const, pipeline_mode=single),
            pl.BlockSpec((1, D_MODEL), const),
            pl.BlockSpec((D_MODEL, D_FF), const, pipeline_mode=single),
            pl.BlockSpec((D_FF, D_MODEL), const, pipeline_mode=single),
            pl.BlockSpec((1, D_MODEL), const),
        ],
        out_specs=pl.BlockSpec((tm, D_MODEL), lambda i: (i, 0)),
        out_shape=jax.ShapeDtypeStruct((n, D_MODEL), F32),
        compiler_params=pltpu.CompilerParams(
            dimension_semantics=("arbitrary",),
            vmem_limit_bytes=VMEM_LIMIT_BYTES),
        name="post_mixer",
    )(x2, oa, ob, proj, proj, wa, wb, wo, nf, w1, w2, fw)


def kernel(x, positions, norm_mix_w, w_in, ml_gate_b, conv_w, conv_b, da_lambda, da_subln_w,
           ml_norm_w, w_proj_a, w_proj_b, w_out, norm_ffn_w, w_ff1, w_ff2, final_norm_w):
    batch, seq, _ = x.shape
    n = batch * seq
    depth = w_in.shape[0]
    assert seq % ATT_T == 0 and seq % PROJ_TM == 0 and n % POST_TM == 0

    da_w = DA_HEADS * 2 * DA_HEAD_DIM
    ml_qk = ML_HEADS * ML_QK_DIM
    ml_v = ML_HEADS * ML_V_DIM
    o_mq = 3 * da_w
    o_mv = o_mq + 2 * ml_qk
    o_gi = o_mv + ml_v
    o_mo = o_gi + 2 * ML_HEADS

    pos2 = jnp.repeat(positions.reshape(n), ROPE_DIM // 2).reshape(-1, LANES)
    inv = ROPE_THETA ** (-jnp.arange(0, ROPE_DIM, 2, dtype=F32) / ROPE_DIM)
    invf = jnp.tile(inv, LANES // (ROPE_DIM // 2)).reshape(1, LANES)

    h = x.reshape(n, D_MODEL)
    for l in range(depth):
        lambda_init = 0.8 - 0.6 * math.exp(-0.3 * l)
        w = w_in[l].astype(BF16)
        wa = w
        wb = w[:, o_mo:]
        wg = jnp.pad(w[:, o_gi:o_mo], ((0, 0), (0, LANES - 2 * ML_HEADS)))
        gate_b = jnp.pad(ml_gate_b[l], (0, LANES - 2 * ML_HEADS)).reshape(1, LANES)

        proj, gates = _inproj(h, pos2, norm_mix_w[l].reshape(1, D_MODEL), invf, conv_w[l],
                              conv_b[l].reshape(1, -1), wa, wb, wg, seq)
        oa, ob = _mixers(proj, gates, da_lambda[l], da_subln_w[l].reshape(1, DA_V_DIM), gate_b,
                         ml_norm_w[l].reshape(1, -1), batch, seq, lambda_init)
        h = _post(h, oa, ob, proj,
                  w_proj_a[l].astype(BF16), w_proj_b[l].astype(BF16), w_out[l].astype(BF16),
                  norm_ffn_w[l].reshape(1, D_MODEL), w_ff1[l].astype(BF16),
                  w_ff2[l].astype(BF16), final_norm_w.reshape(1, D_MODEL),
                  final_norm=(l == depth - 1))
    return h.reshape(batch, seq, D_MODEL)
```

```python
import functools
import math

import jax
import jax.numpy as jnp
from jax import lax
from jax.experimental import pallas as pl
from jax.experimental.pallas import tpu as pltpu

F32 = jnp.float32
BF16 = jnp.bfloat16

D_MODEL = 1024
DA_HEADS = 8
DA_HEAD_DIM = 64
DA_V_DIM = 128
ROPE_DIM = 16
ROPE_THETA = 500000.0
ML_HEADS = 8
ML_QK_DIM = 64
ML_V_DIM = 128
CONV_WIDTH = 4
D_FF = 4 * D_MODEL
EPS = 1e-6

LANES = 128
SUBLANES = 8
VMEM_LIMIT_BYTES = 56 * 1024 * 1024

PROJ_TM = 512
PROJ_TN = 1024
PROJ_CN = 256
ATT_T = 512
ATT_STEPS_PER_REGION = 9
ML_CHUNK = 256
POST_TM = 512
FF_CHUNK = 1024

SEG_Q, SEG_K, SEG_V, SEG_MQK, SEG_MV, SEG_MO, SEG_G0, SEG_G1 = range(8)
N_SEG = 8

NEG_BIG = -1e30
LOG2E = 1.4426950408889634


def _dot(a, b):
    return jnp.dot(a, b, preferred_element_type=F32)


def _dot_nt(a, b):
    return lax.dot_general(a, b, (((1,), (1,)), ((), ())), preferred_element_type=F32)


def _sigmoid(x):
    return 0.5 * jnp.tanh(0.5 * x) + 0.5


def _rms(x, w):
    return x * lax.rsqrt(jnp.mean(x * x, axis=-1, keepdims=True) + EPS) * w


def _inproj_kernel(x_ref, posd_ref, nw_ref, invf_ref, cw_ref, cb_ref, wa_ref, wb_ref, wg_ref,
                   out_ref, gates_ref, cbuf, xn_sc, *, tiles_per_seq):
    tm = PROJ_TM
    cn = PROJ_CN
    i = pl.program_id(0)

    hist_rows = CONV_WIDTH - 1
    first = i % tiles_per_seq == 0

    @pl.when(first)
    def _():
        cbuf[0:SUBLANES, :] = jnp.zeros((SUBLANES, PROJ_TN), F32)

    @pl.when(jnp.logical_not(first))
    def _():
        cbuf[0:SUBLANES, :] = cbuf[tm:tm + SUBLANES, :]

    xn_sc[...] = _rms(x_ref[...], nw_ref[...]).astype(BF16)
    gates_ref[...] = _dot(xn_sc[...], wg_ref[...])
    half = ROPE_DIM // 2

    def chunk(seg, c_in_seg, tables):
        c0 = seg * PROJ_TN + c_in_seg
        if seg < SEG_MO:
            w_chunk = wa_ref[:, c0:c0 + cn]
        else:
            w_chunk = wb_ref[:, c0 - SEG_MO * PROJ_TN:c0 - SEG_MO * PROJ_TN + cn]
        acc = _dot(xn_sc[...], w_chunk)
        if seg in (SEG_Q, SEG_K):
            cos, sin_signed, partner_lane = tables
            scale = DA_HEAD_DIM ** -0.5 * LOG2E if seg == SEG_Q else 1.0
            for l0 in range(0, cn, LANES):
                xc = acc[:, l0:l0 + LANES]
                r = xc * cos + jnp.take_along_axis(xc, partner_lane, axis=1) * sin_signed
                out_ref[:, c0 + l0:c0 + l0 + LANES] = (r * scale).astype(BF16)
        elif seg == SEG_MQK:
            m0 = c0 - SEG_MQK * PROJ_TN
            cbuf[SUBLANES:SUBLANES + tm, m0:m0 + cn] = acc
            for l0 in range(0, cn, LANES):
                ms = slice(m0 + l0, m0 + l0 + LANES)
                conv = cb_ref[:, ms]
                for j in range(CONV_WIDTH):
                    off = SUBLANES - hist_rows + j
                    conv = conv + cw_ref[j:j + 1, ms] * cbuf[off:off + tm, ms]
                y = conv * _sigmoid(conv)
                if m0 < ML_HEADS * ML_QK_DIM:
                    y = y * (ML_QK_DIM ** -0.5)
                out_ref[:, c0 + l0:c0 + l0 + LANES] = y.astype(BF16)
        elif seg in (SEG_V, SEG_MV):
            out_ref[:, c0:c0 + cn] = acc.astype(BF16)
        else:
            for l0 in range(0, cn, LANES):
                out_ref[:, c0 + l0:c0 + l0 + LANES] = _sigmoid(acc[:, l0:l0 + LANES]).astype(BF16)

    rpd = LANES // half
    ang = posd_ref[...].astype(F32) * invf_ref[...]
    lane_id = lax.broadcasted_iota(jnp.int32, (tm, LANES), 1)
    row_id = lax.broadcasted_iota(jnp.int32, (tm, LANES), 0)
    own = (lane_id // half) == (row_id % rpd)
    expand = ((lax.broadcasted_iota(jnp.int32, (LANES, LANES), 0) % half)
              == (lax.broadcasted_iota(jnp.int32, (LANES, LANES), 1) % half)).astype(BF16)

    def expand_rows(d):
        rep = jnp.concatenate([jnp.broadcast_to(d[r:r + 1, :], (rpd, LANES))
                               for r in range(tm // rpd)], axis=0)
        a = jnp.where(own, rep, 0.0)
        hi = a.astype(BF16)
        lo = (a - hi.astype(F32)).astype(BF16)
        return _dot(hi, expand) + _dot(lo, expand)

    c = expand_rows(jnp.cos(ang))
    s = expand_rows(jnp.sin(ang))
    sub = lane_id % DA_HEAD_DIM
    partner_lane = jnp.where(sub < half, lane_id + half,
                             jnp.where(sub < ROPE_DIM, lane_id - half, lane_id))
    tables = (jnp.where(sub < ROPE_DIM, c, 1.0),
              jnp.where(sub < half, -s, jnp.where(sub < ROPE_DIM, s, 0.0)), partner_lane)
    for c_in_seg in range(0, PROJ_TN, cn):
        for seg in (SEG_V, SEG_Q, SEG_MV, SEG_MQK, SEG_G0, SEG_K, SEG_MO, SEG_G1):
            chunk(seg, c_in_seg, tables)


def _inproj(x2, pos2, nw, invf, conv_w, conv_b, wa, wb, wg, seq):
    n = x2.shape[0]
    tm = PROJ_TM
    const = lambda i: (0, 0)
    single = pl.Buffered(1)
    kern = functools.partial(_inproj_kernel, tiles_per_seq=seq // tm)
    return pl.pallas_call(
        kern,
        grid=(n // tm,),
        in_specs=[
            pl.BlockSpec((tm, D_MODEL), lambda i: (i, 0)),
            pl.BlockSpec((tm // (LANES // (ROPE_DIM // 2)), LANES), lambda i: (i, 0)),
            pl.BlockSpec((1, D_MODEL), const),
            pl.BlockSpec((1, LANES), const),
            pl.BlockSpec((CONV_WIDTH, PROJ_TN), const),
            pl.BlockSpec((1, PROJ_TN), const),
            pl.BlockSpec((D_MODEL, SEG_MO * PROJ_TN), const, pipeline_mode=single),
            pl.BlockSpec((D_MODEL, (N_SEG - SEG_MO) * PROJ_TN), const, pipeline_mode=single),
            pl.BlockSpec((D_MODEL, LANES), const, pipeline_mode=single),
        ],
        out_specs=[
            pl.BlockSpec((tm, N_SEG * PROJ_TN), lambda i: (i, 0)),
            pl.BlockSpec((tm, LANES), lambda i: (i, 0)),
        ],
        out_shape=[
            jax.ShapeDtypeStruct((n, N_SEG * PROJ_TN), BF16),
            jax.ShapeDtypeStruct((n, LANES), F32),
        ],
        scratch_shapes=[
            pltpu.VMEM((tm + SUBLANES, PROJ_TN), F32),
            pltpu.VMEM((tm, D_MODEL), BF16),
        ],
        compiler_params=pltpu.CompilerParams(
            dimension_semantics=("arbitrary",),
            vmem_limit_bytes=VMEM_LIMIT_BYTES),
        name="inproj",
    )(x2, pos2, nw, invf, conv_w, conv_b, wa, wb, wg)


def _mixers_kernel(lam_ref, sw_ref, q_ref, k_ref, v_ref, mqk_ref, mv_ref, mso_ref, mg_ref,
                   mgb_ref, mnw_ref, o_ref, ob_ref, vext_sc, sa_sc, sb_sc, pa_sc, pb_sc, acc_sc,
                   m_sc, cext_sc, mm_sc, mask_sc, *, seq, lambda_init):
    ml_init, ml_gate, ml_head = _mlstm_parts(mqk_ref, mv_ref, mso_ref, mg_ref, mgb_ref, mnw_ref,
                                             ob_ref, cext_sc, mm_sc, mask_sc)

    @pl.when(pl.program_id(1) == 0)
    def _():
        ml_init()

    t = ATT_T
    hd = DA_HEAD_DIM
    dv = DA_V_DIM
    nq = seq // t
    rows = 2 * t

    @pl.when((pl.program_id(0) == 0) & (pl.program_id(1) == 0))
    def _():
        vext_sc[:, dv:] = jnp.ones((seq, LANES), BF16)

    vext_sc[:, :dv] = v_ref[...]

    lp = lam_ref[...]
    lam = (jnp.exp(jnp.sum(lp[0:1] * lp[1:2], axis=-1, keepdims=True))
           - jnp.exp(jnp.sum(lp[2:3] * lp[3:4], axis=-1, keepdims=True)) + lambda_init)

    def blk(i):
        return pl.ds(i * t if isinstance(i, int) else pl.multiple_of(i * t, t), t)

    def split_q(qi):
        q = q_ref[blk(qi), :]
        lane = lax.broadcasted_iota(jnp.int32, (t, LANES), 1)
        zero = jnp.zeros_like(q)
        return jnp.where(lane < hd, q, zero), jnp.where(lane >= hd, q, zero)

    def lane_tile_max(s):
        pm = s[:, 0:LANES]
        for c in range(1, s.shape[1] // LANES):
            pm = jnp.maximum(pm, s[:, c * LANES:(c + 1) * LANES])
        return pm

    def exp_tiles(src, r0, r1, width, m):
        return jnp.concatenate(
            [jnp.exp2(src[r0:r1, c * LANES:(c + 1) * LANES] - m).astype(BF16)
             for c in range(width // LANES)], axis=1)

    def scores(dst_s, dst_pm, qi, kblk):
        q0, q1 = split_q(qi)
        s = _dot_nt(jnp.concatenate([q0, q1], axis=0), k_ref[blk(kblk), :])
        dst_s[...] = s
        dst_pm[...] = lane_tile_max(s)

    def process(src_s, src_pm, qi, vblk):
        m_prev = m_sc[qi]
        m_new = jnp.maximum(m_prev, jnp.max(src_pm[...], axis=-1, keepdims=True))
        alpha = jnp.exp2(m_prev - m_new)
        p = exp_tiles(src_s, 0, rows, t, m_new)
        pv = _dot(p, vext_sc[blk(vblk), :])
        acc_sc[qi] = jnp.concatenate([alpha, alpha], axis=1) * acc_sc[qi] + pv
        m_sc[qi] = m_new

    hh = t // 2

    def scores_diag(dst_s, dst_pm, qi):
        q0, q1 = split_q(qi)
        k_all = k_ref[qi * t:(qi + 1) * t, :]
        s_a = _dot_nt(jnp.concatenate([q0[:hh], q1[:hh]], axis=0), k_all[:hh])
        s_b = _dot_nt(jnp.concatenate([q0[hh:], q1[hh:]], axis=0), k_all)
        keep_a = (lax.broadcasted_iota(jnp.int32, (hh, hh), 1)
                  <= lax.broadcasted_iota(jnp.int32, (hh, hh), 0))
        keep_b = (lax.broadcasted_iota(jnp.int32, (hh, t), 1)
                  <= lax.broadcasted_iota(jnp.int32, (hh, t), 0) + hh)
        s_a = jnp.where(jnp.concatenate([keep_a, keep_a], axis=0), s_a, NEG_BIG)
        s_b = jnp.where(jnp.concatenate([keep_b, keep_b], axis=0), s_b, NEG_BIG)
        dst_s[0:t, 0:hh] = s_a
        dst_s[t:rows, :] = s_b
        dst_pm[0:t, :] = lane_tile_max(s_a)
        dst_pm[t:rows, :] = lane_tile_max(s_b)

    def process_diag(src_s, src_pm, qi):
        m_a = jnp.broadcast_to(jnp.max(src_pm[0:t, :], axis=-1, keepdims=True), (t, LANES))
        m_b = jnp.broadcast_to(jnp.max(src_pm[t:rows, :], axis=-1, keepdims=True), (t, LANES))
        pv_a = _dot(exp_tiles(src_s, 0, t, hh, m_a), vext_sc[qi * t:qi * t + hh, :])
        pv_b = _dot(exp_tiles(src_s, t, rows, t, m_b), vext_sc[qi * t:(qi + 1) * t, :])
        for dst, val_acc, val_m in ((0, pv_a[:hh], m_a[:hh]), (hh, pv_b[:hh], m_b[:hh]),
                                    (t, pv_a[hh:], m_a[hh:]), (t + hh, pv_b[hh:], m_b[hh:])):
            acc_sc[qi, dst:dst + hh, :] = val_acc
            m_sc[qi, dst:dst + hh, :] = val_m

    def finalize(qi):
        acc = acc_sc[qi]
        o = acc[:, :dv] / acc[:, dv:]
        od = o[:t] - lam * o[t:]
        y = _rms(od, sw_ref[...]) * (1.0 - lambda_init)
        o_ref[qi * t:(qi + 1) * t, :] = y.astype(BF16)

    bufs = ((sa_sc, pa_sc), (sb_sc, pb_sc))
    steps = []
    for qi in range(nq):
        steps.append((qi, qi, True))
        steps.extend((qi, j, False) for j in range(qi))
    def emit_scores(g):
        qi, kblk, diag = steps[g]
        if diag:
            scores_diag(*bufs[g % 2], qi)
        else:
            scores(*bufs[g % 2], qi, kblk)

    emit_scores(0)

    def emit(g0, g1, ml_chunk):
        ctx = None
        heads_done = 0
        every = max(1, (g1 - g0) // (ML_HEADS + 1))
        for g in range(g0, g1):
            if ml_chunk is not None and (g - g0) % every == 0:
                if ctx is None:
                    ctx = ml_gate(ml_chunk * ML_CHUNK)
                elif heads_done < ML_HEADS:
                    ml_head(ctx, heads_done)
                    heads_done += 1
            qi, kblk, diag = steps[g]
            if g + 1 < len(steps):
                emit_scores(g + 1)
            if diag:
                process_diag(*bufs[g % 2], qi)
            else:
                process(*bufs[g % 2], qi, kblk)
            if g + 1 == len(steps) or steps[g + 1][0] != qi:
                finalize(qi)
        if ml_chunk is not None:
            for h in range(heads_done, ML_HEADS):
                ml_head(ctx, h)

    one = jnp.minimum(pl.program_id(0), 0) + 1
    starts = list(range(0, len(steps), ATT_STEPS_PER_REGION))
    ml_chunks = t // ML_CHUNK
    assert len(starts) % ml_chunks == 0
    for r, g0 in enumerate(starts):
        g1 = min(g0 + ATT_STEPS_PER_REGION, len(steps))
        stride = len(starts) // ml_chunks
        ml_chunk = r // stride if r % stride == 0 else None

        def region(_, c, g0=g0, g1=g1, ml_chunk=ml_chunk):
            emit(g0, g1, ml_chunk)
            return c

        lax.fori_loop(0, one, region, 0)


def _mixers(proj, gates, lam_p, subln_w, gate_b, ml_norm_w, batch, seq, lambda_init):
    n = proj.shape[0]
    t = ATT_T
    nq = seq // t
    assert nq == DA_HEADS and t % ML_CHUNK == 0
    kern = functools.partial(_mixers_kernel, seq=seq, lambda_init=lambda_init)
    hb = PROJ_TN // LANES
    mw = ML_HEADS * ML_V_DIM
    const = lambda b, h: (0, 0)
    return pl.pallas_call(
        kern,
        grid=(batch, DA_HEADS),
        in_specs=[
            pl.BlockSpec((4, DA_HEAD_DIM), const),
            pl.BlockSpec((1, DA_V_DIM), const),
            pl.BlockSpec((seq, LANES), lambda b, h: (b, SEG_Q * hb + h)),
            pl.BlockSpec((seq, LANES), lambda b, h: (b, SEG_K * hb + h)),
            pl.BlockSpec((seq, LANES), lambda b, h: (b, SEG_V * hb + h)),
            pl.BlockSpec((t, PROJ_TN), lambda b, h: (b * nq + h, SEG_MQK)),
            pl.BlockSpec((t, PROJ_TN), lambda b, h: (b * nq + h, SEG_MV)),
            pl.BlockSpec((t, PROJ_TN), lambda b, h: (b * nq + h, SEG_MO)),
            pl.BlockSpec((t, LANES), lambda b, h: (b * nq + h, 0)),
            pl.BlockSpec((1, LANES), const),
            pl.BlockSpec((1, mw), const),
        ],
        out_specs=[
            pl.BlockSpec((seq, LANES), lambda b, h: (b, h)),
            pl.BlockSpec((t, mw), lambda b, h: (b * nq + h, 0)),
        ],
        out_shape=[
            jax.ShapeDtypeStruct((n, DA_HEADS * DA_V_DIM), BF16),
            jax.ShapeDtypeStruct((n, mw), BF16),
        ],
        scratch_shapes=[
            pltpu.VMEM((seq, 2 * LANES), BF16),
            pltpu.VMEM((2 * t, t), F32),
            pltpu.VMEM((2 * t, t), F32),
            pltpu.VMEM((2 * t, LANES), F32),
            pltpu.VMEM((2 * t, LANES), F32),
            pltpu.VMEM((nq, 2 * t, 2 * LANES), F32),
            pltpu.VMEM((nq, 2 * t, LANES), F32),
            pltpu.VMEM((ML_HEADS, ML_QK_DIM, 2 * LANES), F32),
            pltpu.VMEM((SUBLANES, LANES), F32),
            pltpu.VMEM((ML_CHUNK, ML_CHUNK), F32),
        ],
        compiler_params=pltpu.CompilerParams(
            dimension_semantics=("arbitrary", "arbitrary"),
            vmem_limit_bytes=VMEM_LIMIT_BYTES),
        name="mixers",
    )(lam_p, subln_w, proj, proj, proj, proj, proj, proj, gates, gate_b, ml_norm_w)


def _mlstm_parts(qk_ref, v_ref, so_ref, g_ref, gb_ref, nw_ref, out_ref, cext_sc, m_sc, mask_sc):
    L = ML_CHUNK
    dk, dv = ML_QK_DIM, ML_V_DIM
    qkw = ML_HEADS * dk
    nlt = L // LANES

    def init():
        cext_sc[...] = jnp.zeros(cext_sc.shape, F32)
        m_sc[...] = jnp.zeros(m_sc.shape, F32)
        row = lax.broadcasted_iota(jnp.int32, (L, L), 0)
        col = lax.broadcasted_iota(jnp.int32, (L, L), 1)
        mask_sc[...] = jnp.where(col <= row, 0.0, NEG_BIG)

    def gate(r0):
        rs = slice(r0, r0 + L)
        tri_b = (mask_sc[...] == 0.0).astype(BF16)
        rowi = lax.broadcasted_iota(jnp.int32, (L, LANES), 0)
        g = g_ref[rs, :] + gb_ref[...]
        logf = jnp.minimum(g, 0.0) - jnp.log(1.0 + jnp.exp(-jnp.abs(g)))
        hi = logf.astype(BF16)
        r1 = logf - hi.astype(F32)
        mid = r1.astype(BF16)
        lo = (r1 - mid.astype(F32)).astype(BF16)
        bcs = _dot(tri_b, hi) + _dot(tri_b, mid) + _dot(tri_b, lo)
        b_al = pltpu.roll(bcs, LANES - ML_HEADS, 1)
        a = g - b_al
        m_prev = m_sc[0:1, :]
        cm = a
        d = 1
        while d < L:
            cm = jnp.maximum(cm, jnp.where(rowi >= d, pltpu.roll(cm, d, 0), NEG_BIG))
            d *= 2
        u = jnp.maximum(cm, m_prev)
        b_last = b_al[L - 1:L, :]
        w_log = b_last + a
        m_new = jnp.maximum(b_last + m_prev, jnp.max(w_log, axis=0, keepdims=True))
        m_sc[0:1, :] = m_new
        return dict(rs=rs, u=u, mt=b_al + u, m_prev=m_prev,
                    decay=jnp.exp(b_last + m_prev - m_new),
                    a_t=a.T,
                    ws_t=jnp.exp(w_log - m_new).T)

    def head(ctx, h):
        rs = ctx["rs"]
        u_b = jnp.broadcast_to(ctx["u"][:, h:h + 1], (L, LANES))
        mt_b = jnp.broadcast_to(ctx["mt"][:, h:h + 1], (L, LANES))
        inter_b = jnp.exp(ctx["m_prev"][:, h:h + 1] - u_b)
        floor_b = jnp.exp(-mt_b)
        dexp = jnp.concatenate(
            [jnp.exp(ctx["a_t"][h:h + 1, t * LANES:(t + 1) * LANES] - u_b
                     + mask_sc[:, t * LANES:(t + 1) * LANES]) for t in range(nlt)], axis=1)
        qh = qk_ref[rs, h * dk:(h + 1) * dk]
        kh_t = qk_ref[rs, qkw + h * dk:qkw + (h + 1) * dk].T
        cext = cext_sc[h]
        vext = jnp.concatenate([v_ref[rs, h * dv:(h + 1) * dv], jnp.ones((L, LANES), BF16)],
                               axis=1)
        s = (_dot(qh, kh_t) * dexp).astype(BF16)
        hext = (_dot(s, vext)
                + jnp.concatenate([inter_b, inter_b], axis=1) * _dot(qh, cext.astype(BF16)))
        hm = hext[:, :dv] / jnp.maximum(jnp.abs(hext[:, dv:]), floor_b)
        y = _rms(hm, nw_ref[:, h * dv:(h + 1) * dv])
        y = so_ref[rs, h * dv:(h + 1) * dv].astype(F32) * y
        out_ref[rs, h * dv:(h + 1) * dv] = y.astype(BF16)
        kw_t = (kh_t.astype(F32) * ctx["ws_t"][h:h + 1, :]).astype(BF16)
        cext_sc[h] = ctx["decay"][:, h:h + 1] * cext + _dot(kw_t, vext)

    return init, gate, head


def _post_kernel(x_ref, oa_ref, ob_ref, g0_ref, g1_ref, wa_ref, wb_ref, wo_ref, nf_ref,
                 w1_ref, w2_ref, fw_ref, out_ref, *, final_norm):
    ya = _dot(oa_ref[...], wa_ref[...])
    yb = _dot(ob_ref[...], wb_ref[...])
    merged = g0_ref[...].astype(F32) * ya + g1_ref[...].astype(F32) * yb
    h = x_ref[...] + _dot(merged.astype(BF16), wo_ref[...])
    hn = _rms(h, nf_ref[...]).astype(BF16)
    acc = h
    for c0 in range(0, D_FF, FF_CHUNK):
        u = jnp.maximum(_dot(hn, w1_ref[:, c0:c0 + FF_CHUNK]), 0.0)
        acc = acc + _dot((u * u).astype(BF16), w2_ref[c0:c0 + FF_CHUNK, :])
    if final_norm:
        acc = _rms(acc, fw_ref[...])
    out_ref[...] = acc


def _post(x2, oa, ob, proj, wa, wb, wo, nf, w1, w2, fw, final_norm):
    n = x2.shape[0]
    tm = POST_TM
    const = lambda i: (0, 0)
    single = pl.Buffered(1)
    kern = functools.partial(_post_kernel, final_norm=final_norm)
    return pl.pallas_call(
        kern,
        grid=(n // tm,),
        in_specs=[
            pl.BlockSpec((tm, D_MODEL), lambda i: (i, 0)),
            pl.BlockSpec((tm, D_MODEL), lambda i: (i, 0)),
            pl.BlockSpec((tm, D_MODEL), lambda i: (i, 0)),
            pl.BlockSpec((tm, PROJ_TN), lambda i: (i, SEG_G0)),
            pl.BlockSpec((tm, PROJ_TN), lambda i: (i, SEG_G1)),
            pl.BlockSpec((D_MODEL, D_MODEL), const, pipeline_mode=single),
            pl.BlockSpec((D_MODEL, D_MODEL), const, pipeline_mode=single),
            pl.BlockSpec((D_MODEL, D_MODEL), const, pipeline_mode=single),
            pl.BlockSpec((1, D_MODEL), const),
            pl.BlockSpec((D_MODEL, D_FF), const, pipeline_mode=single),
            pl.BlockSpec((D_FF, D_MODEL), const, pipeline_mode=single),
            pl.BlockSpec((1, D_MODEL), const),
        ],
        out_specs=pl.BlockSpec((tm, D_MODEL), lambda i: (i, 0)),
        out_shape=jax.ShapeDtypeStruct((n, D_MODEL), F32),
        compiler_params=pltpu.CompilerParams(
            dimension_semantics=("arbitrary",),
            vmem_limit_bytes=VMEM_LIMIT_BYTES),
        name="post_mixer",
    )(x2, oa, ob, proj, proj, wa, wb, wo, nf, w1, w2, fw)


def kernel(x, positions, norm_mix_w, w_in, ml_gate_b, conv_w, conv_b, da_lambda, da_subln_w,
           ml_norm_w, w_proj_a, w_proj_b, w_out, norm_ffn_w, w_ff1, w_ff2, final_norm_w):
    batch, seq, _ = x.shape
    n = batch * seq
    depth = w_in.shape[0]
    assert seq % ATT_T == 0 and seq % PROJ_TM == 0 and n % POST_TM == 0

    da_w = DA_HEADS * 2 * DA_HEAD_DIM
    ml_qk = ML_HEADS * ML_QK_DIM
    ml_v = ML_HEADS * ML_V_DIM
    o_mq = 3 * da_w
    o_mv = o_mq + 2 * ml_qk
    o_gi = o_mv + ml_v
    o_mo = o_gi + 2 * ML_HEADS

    pos2 = jnp.repeat(positions.reshape(n), ROPE_DIM // 2).reshape(-1, LANES)
    inv = ROPE_THETA ** (-jnp.arange(0, ROPE_DIM, 2, dtype=F32) / ROPE_DIM)
    invf = jnp.tile(inv, LANES // (ROPE_DIM // 2)).reshape(1, LANES)

    h = x.reshape(n, D_MODEL)
    for l in range(depth):
        lambda_init = 0.8 - 0.6 * math.exp(-0.3 * l)
        w = w_in[l].astype(BF16)
        wa = w
        wb = w[:, o_mo:]
        wg = jnp.pad(w[:, o_gi:o_mo], ((0, 0), (0, LANES - 2 * ML_HEADS)))
        gate_b = jnp.pad(ml_gate_b[l], (0, LANES - 2 * ML_HEADS)).reshape(1, LANES)

        proj, gates = _inproj(h, pos2, norm_mix_w[l].reshape(1, D_MODEL), invf, conv_w[l],
                              conv_b[l].reshape(1, -1), wa, wb, wg, seq)
        oa, ob = _mixers(proj, gates, da_lambda[l], da_subln_w[l].reshape(1, DA_V_DIM), gate_b,
                         ml_norm_w[l].reshape(1, -1), batch, seq, lambda_init)
        h = _post(h, oa, ob, proj,
                  w_proj_a[l].astype(BF16), w_proj_b[l].astype(BF16), w_out[l].astype(BF16),
                  norm_ffn_w[l].reshape(1, D_MODEL), w_ff1[l].astype(BF16),
                  w_ff2[l].astype(BF16), final_norm_w.reshape(1, D_MODEL),
                  final_norm=(l == depth - 1))
    return h.reshape(batch, seq, D_MODEL)
```
